```python
import math
import jax
import jax.numpy as jnp
from jax import lax
import numpy as np

D_MODEL = 4096
BATCH = 4
SEQ = 2048
DEPTH = 2
DEC_BATCH = 8
DEC_SEQ = 8
PAST_LEN = 16384
PAGE_SIZE = 128

HA = 8
DH = 128
HB = 16
KVB = 2
GB = HB // KVB
H_IDX = 32
D_IDX = 128
TOPK_MAX = 256
IDX_SCALE = (H_IDX * D_IDX) ** -0.5
Q_BLOCK = 128
N_BUCKETS = 32
BUCKET_EXACT = 16
BUCKET_MAX_DIST = 128
RW_N = 64
RW_H = D_MODEL // RW_N
LORA_W = 128
LORA_A = 128
LORA_G = 480
RW_GN_EPS = 64e-5
N_EXPERTS = 16
N_GROUPS = 4
EXP_PER_GROUP = N_EXPERTS // N_GROUPS
TOP_K_EXPERTS = 2
D_FF = 1024
LN_EPS = 1e-5
ALPHA = (2 * DEPTH) ** 0.25
BETA = (8 * DEPTH) ** -0.25
N_ATT_LAYERS = (DEPTH + 1) // 2
N_RWKV_LAYERS = DEPTH // 2
_SPLIT_SIZES = (HA * 2 * DH, HA * 2 * DH, HA * 2 * DH, HB * DH, KVB * DH, KVB * DH, H_IDX * D_IDX, D_IDX, H_IDX)
SPLIT_POINTS = tuple(int(v) for v in np.cumsum(_SPLIT_SIZES)[:-1])
IN_WIDTH = sum(_SPLIT_SIZES)
ATT_OUT = HA * 2 * DH + HB * DH

kernel_name = 'hybrid_diffattn_dsa_rwkv7_moe_step'


def layer_norm(x, g, b):
    xf = x.astype(jnp.float32)
    mu = jnp.mean(xf, -1, keepdims=True)
    var = jnp.mean(jnp.square(xf - mu), -1, keepdims=True)
    return ((xf - mu) * lax.rsqrt(var + LN_EPS) * g + b).astype(x.dtype)


def t5_bucket(dist):
    n = jnp.maximum(dist, 0)
    nf = jnp.maximum(n, 1).astype(jnp.float32)
    large = BUCKET_EXACT + (jnp.log(nf / BUCKET_EXACT) / math.log(BUCKET_MAX_DIST / BUCKET_EXACT)
                            * (N_BUCKETS - BUCKET_EXACT)).astype(jnp.int32)
    large = jnp.minimum(large, N_BUCKETS - 1)
    return jnp.where(n < BUCKET_EXACT, n, large)


def gather_rows(table, idx):
    return jax.vmap(lambda t, i: t[i])(table, idx)


def to_blocks(t):
    b, s = t.shape[:2]
    return jnp.moveaxis(t.reshape(b, s // Q_BLOCK, Q_BLOCK, *t.shape[2:]), 1, 0)


def att_project(x, w_in):
    b, t, _ = x.shape
    h = jnp.einsum('btd,dc->btc', x, w_in)
    qa, ka, va, qb, kb, vb, qi, ki, wi = jnp.split(h, SPLIT_POINTS, axis=-1)
    return (qa.reshape(b, t, HA, 2, DH), ka.reshape(b, t, HA, 2, DH), va.reshape(b, t, HA, 2 * DH),
            qb.reshape(b, t, HB, DH), kb.reshape(b, t, KVB, DH), vb.reshape(b, t, KVB, DH),
            qi.reshape(b, t, H_IDX, D_IDX), ki, wi)


def diff_lambda(lq1, lk1, lq2, lk2, layer_idx):
    lam_init = 0.8 - 0.6 * math.exp(-0.3 * layer_idx)
    lam = (jnp.exp(jnp.sum(lq1.astype(jnp.float32) * lk1.astype(jnp.float32)))
           - jnp.exp(jnp.sum(lq2.astype(jnp.float32) * lk2.astype(jnp.float32))) + lam_init)
    return lam, lam_init


def diff_probs(q, k, bias, mask):
    s = jnp.einsum('bqhcd,bkhcd->bhcqk', q, k).astype(jnp.float32) * (DH ** -0.5)
    s = jnp.where(mask, s + bias[None, :, None].astype(jnp.float32), -jnp.inf)
    return jax.nn.softmax(s, axis=-1)


def diff_combine(p, v, lam, lam_init, subln_g):
    attn = p[:, :, 0] - lam * p[:, :, 1]
    o = jnp.einsum('bhqk,bkhe->bqhe', attn, v.astype(jnp.float32))
    o = o * lax.rsqrt(jnp.mean(o * o, -1, keepdims=True) + 1e-5)
    return o * subln_g.astype(jnp.float32) * (1.0 - lam_init)


def indexer_scores(qi, wi, ki):
    dots = jnp.einsum('bqhd,bkd->bqhk', qi, ki).astype(jnp.float32)
    return jnp.einsum('bqhk,bqh->bqk', jax.nn.relu(dots), wi.astype(jnp.float32)) * IDX_SCALE


def sparse_attend(qb, k_sel, v_sel, sel_dist, bias_b):
    b, q = qb.shape[:2]
    kk = k_sel.shape[2]
    qg = qb.reshape(b, q, KVB, GB, DH)
    s = jnp.einsum('bqgrd,bqkgd->bqgrk', qg, k_sel).astype(jnp.float32) * (DH ** -0.5)
    bias = bias_b[t5_bucket(sel_dist)].reshape(b, q, kk, KVB, GB).transpose(0, 1, 3, 4, 2)
    valid = (sel_dist >= 0)[:, :, None, None, :]
    s = jnp.where(valid, s + bias.astype(jnp.float32), -jnp.inf)
    p = jax.nn.softmax(s, axis=-1)
    o = jnp.einsum('bqgrk,bqkgd->bqgrd', p, v_sel.astype(jnp.float32))
    return o.reshape(b, q, HB * DH)


def att_mixer_prompt(x, w_in, w_out, rel_bias, lam, lam_init, subln_g):
    b, s, _ = x.shape
    qa, ka, va, qb, kb, vb, qi, ki, wi = att_project(x, w_in)
    nblk = s // Q_BLOCK
    topk = min(TOPK_MAX, s // 4)
    kpos = jnp.arange(s)
    bias_a_tab = rel_bias[:, :HA]
    bias_b_tab = rel_bias[:, HA:]

    def diff_block(args):
        blk, q_blk = args
        qpos = blk * Q_BLOCK + jnp.arange(Q_BLOCK)
        dist = qpos[:, None] - kpos[None, :]
        bias = bias_a_tab[t5_bucket(dist)].transpose(2, 0, 1)
        p = diff_probs(q_blk, ka, bias, dist >= 0)
        return diff_combine(p, va, lam, lam_init, subln_g)

    oa = lax.map(diff_block, (jnp.arange(nblk), to_blocks(qa)))
    oa = jnp.moveaxis(oa, 0, 1).reshape(b, s, HA * 2 * DH)

    def sparse_block(args):
        blk, qb_blk, qi_blk, wi_blk = args
        qpos = blk * Q_BLOCK + jnp.arange(Q_BLOCK)
        isc = indexer_scores(qi_blk, wi_blk, ki)
        isc = jnp.where((kpos[None, :] <= qpos[:, None])[None], isc, -jnp.inf)
        _, sel = lax.top_k(isc, topk)
        k_sel = gather_rows(kb, sel)
        v_sel = gather_rows(vb, sel)
        return sparse_attend(qb_blk, k_sel, v_sel, qpos[None, :, None] - sel, bias_b_tab)

    ob = lax.map(sparse_block, (jnp.arange(nblk), to_blocks(qb), to_blocks(qi), to_blocks(wi)))
    ob = jnp.moveaxis(ob, 0, 1).reshape(b, s, HB * DH)
    o = jnp.concatenate([oa, ob], axis=-1).astype(x.dtype)
    return jnp.einsum('btc,cd->btd', o, w_out), (ka, va, kb, vb, ki)


def att_mixer_sample(x, j, cache_ka, cache_va, cache_kb, cache_vb, cache_kidx, page_table,
                     w_in, w_out, rel_bias, lam, lam_init, subln_g):
    bd, t, _ = x.shape
    n_pages = page_table.shape[1]
    past_len = n_pages * PAGE_SIZE
    total = past_len + t
    topk = min(TOPK_MAX, total // 4)
    qa, ka, va, qb, kb, vb, qi, ki, wi = att_project(x, w_in)
    qpos = past_len + jnp.arange(t)
    kpos = jnp.arange(total)
    dist = qpos[:, None] - kpos[None, :]
    ka_all = jnp.concatenate([cache_ka[j, page_table].reshape(bd, past_len, HA, 2, DH), ka], axis=1)
    va_all = jnp.concatenate([cache_va[j, page_table].reshape(bd, past_len, HA, 2 * DH), va], axis=1)
    bias_a = rel_bias[:, :HA][t5_bucket(dist)].transpose(2, 0, 1)
    p = diff_probs(qa, ka_all, bias_a, dist >= 0)
    oa = diff_combine(p, va_all, lam, lam_init, subln_g).reshape(bd, t, HA * 2 * DH)
    ki_all = jnp.concatenate([cache_kidx[j, page_table].reshape(bd, past_len, D_IDX), ki], axis=1)
    isc = jnp.where((dist >= 0)[None], indexer_scores(qi, wi, ki_all), -jnp.inf)
    _, sel = lax.top_k(isc, topk)
    from_past = (sel < past_len)[..., None, None]
    sel_past = jnp.minimum(sel, past_len - 1)
    phys = gather_rows(page_table, sel_past // PAGE_SIZE)
    off = sel_past % PAGE_SIZE
    sel_new = jnp.clip(sel - past_len, 0, t - 1)
    k_sel = jnp.where(from_past, cache_kb[j, phys, off], gather_rows(kb, sel_new))
    v_sel = jnp.where(from_past, cache_vb[j, phys, off], gather_rows(vb, sel_new))
    ob = sparse_attend(qb, k_sel, v_sel, qpos[None, :, None] - sel, rel_bias[:, HA:])
    o = jnp.concatenate([oa, ob], axis=-1).astype(x.dtype)
    return jnp.einsum('btc,cd->btd', o, w_out), (ka, va, kb, vb, ki)


def rwkv7_mix(x, shift0, wkv0, mix, w0, w1, w2, a0, a1, a2, g1, g2, k_k, k_a, r_k, wr, wk, wv, wo, gn_g, gn_b):
    b, t, d = x.shape
    x_prev = jnp.concatenate([shift0[:, None].astype(x.dtype), x[:, :-1]], axis=1)
    xx = x_prev - x
    xr, xw, xk, xv, xa, xg = [x + xx * mix[c] for c in range(6)]
    r = xr @ wr
    w = -jax.nn.softplus(-(w0 + jnp.tanh(xw @ w1) @ w2)) - 0.5
    k = xk @ wk
    v = xv @ wv
    a = jax.nn.sigmoid(a0 + (xa @ a1) @ a2)
    g = jax.nn.sigmoid(xg @ g1) @ g2

    def heads(z):
        return z.reshape(b, t, RW_H, RW_N).astype(jnp.float32)

    kk = heads(k * k_k)
    kk = kk * lax.rsqrt(jnp.maximum(jnp.sum(kk * kk, -1, keepdims=True), 1e-24))
    k = k * (1.0 + (a - 1.0) * k_a)
    r_h, k_h, v_h, a_h = heads(r), heads(k), heads(v), heads(a)
    decay = jnp.exp(-jnp.exp(heads(w)))

    def step(S, inp):
        r_t, k_t, v_t, kk_t, a_t, d_t = inp
        sa = jnp.einsum('bhij,bhj->bhi', S, -kk_t)
        S = (S * d_t[:, :, None, :] + sa[..., None] * (kk_t * a_t)[:, :, None, :]
             + v_t[..., None] * k_t[:, :, None, :])
        return S, jnp.einsum('bhij,bhj->bhi', S, r_t)

    xs = tuple(jnp.moveaxis(z, 1, 0) for z in (r_h, k_h, v_h, kk, a_h, decay))
    s_fin, o = lax.scan(step, wkv0.astype(jnp.float32), xs)
    o = jnp.moveaxis(o, 0, 1)
    mu = jnp.mean(o, -1, keepdims=True)
    var = jnp.mean(jnp.square(o - mu), -1, keepdims=True)
    o = ((o - mu) * lax.rsqrt(var + RW_GN_EPS)).reshape(b, t, d) * gn_g + gn_b
    o = o + (jnp.sum(r_h * k_h * r_k, -1, keepdims=True) * v_h).reshape(b, t, d)
    y = (o * g).astype(x.dtype) @ wo
    return y, s_fin.astype(wkv0.dtype), x[:, -1]


def moe(x, w_router, router_bias, w_gate, w_up, w_down):
    scores = jax.nn.sigmoid(jnp.einsum('btd,de->bte', x, w_router).astype(jnp.float32))
    sel_scores = scores + router_bias.astype(jnp.float32)
    grp = sel_scores.reshape(*sel_scores.shape[:-1], N_GROUPS, EXP_PER_GROUP)
    grp_score = jnp.sum(lax.top_k(grp, TOP_K_EXPERTS)[0], -1)
    best = jnp.argmax(grp_score, -1)
    in_group = (jnp.arange(N_EXPERTS) // EXP_PER_GROUP) == best[..., None]
    _, top_idx = lax.top_k(jnp.where(in_group, sel_scores, -jnp.inf), TOP_K_EXPERTS)
    top_w = jnp.take_along_axis(scores, top_idx, -1)
    top_w = top_w / jnp.sum(top_w, -1, keepdims=True)
    gates = jnp.sum(jax.nn.one_hot(top_idx, N_EXPERTS, dtype=jnp.float32) * top_w[..., None], axis=-2)
    h = jax.nn.silu(jnp.einsum('btd,edf->btef', x, w_gate)) * jnp.einsum('btd,edf->btef', x, w_up)
    h = h * gates[..., None].astype(h.dtype)
    return jnp.einsum('btef,efd->btd', h, w_down)


def setup_inputs(seed: int = 0) -> dict:
    key = jax.random.key(seed)
    ks = iter(jax.random.split(key, 64))
    f32 = jnp.float32

    def nrm(shape, scale=1.0):
        return jax.random.normal(next(ks), shape, f32) * scale

    def unif(shape, lo, hi):
        return jax.random.uniform(next(ks), shape, f32, lo, hi)

    n_pages = PAST_LEN // PAGE_SIZE
    n_used = DEC_BATCH * n_pages
    n_pool = n_used + n_used // 4
    NA, NR, D = N_ATT_LAYERS, N_RWKV_LAYERS, D_MODEL
    return {
        'x_prompt': nrm((BATCH, SEQ, D)),
        'x_sample': nrm((DEC_BATCH, DEC_SEQ, D)),
        'cache_ka': nrm((NA, n_pool, PAGE_SIZE, HA, 2, DH)),
        'cache_va': nrm((NA, n_pool, PAGE_SIZE, HA, 2 * DH)),
        'cache_kb': nrm((NA, n_pool, PAGE_SIZE, KVB, DH)),
        'cache_vb': nrm((NA, n_pool, PAGE_SIZE, KVB, DH)),
        'cache_kidx': nrm((NA, n_pool, PAGE_SIZE, D_IDX)),
        'state_wkv': nrm((NR, DEC_BATCH, RW_H, RW_N, RW_N), 0.5),
        'state_shift': nrm((NR, DEC_BATCH, D)),
        'page_table': jax.random.permutation(next(ks), n_pool)[:n_used].reshape(DEC_BATCH, n_pages).astype(jnp.int32),
        'rel_bias': nrm((N_BUCKETS, HA + HB), 0.5),
        'w_in_att': nrm((NA, D, IN_WIDTH), D ** -0.5),
        'w_out_att': nrm((NA, ATT_OUT, D), ATT_OUT ** -0.5 * BETA),
        'lam_q1': nrm((NA, DH), 0.1),
        'lam_k1': nrm((NA, DH), 0.1),
        'lam_q2': nrm((NA, DH), 0.1),
        'lam_k2': nrm((NA, DH), 0.1),
        'subln_g': 1.0 + nrm((NA, 2 * DH), 0.02),
        'rw_mix': unif((NR, 6, D), 0.0, 1.0),
        'rw_w0': unif((NR, D), -6.0, -1.0),
        'rw_w1': nrm((NR, D, LORA_W), D ** -0.5),
        'rw_w2': nrm((NR, LORA_W, D), 0.1 * LORA_W ** -0.5),
        'rw_a0': nrm((NR, D), 0.1),
        'rw_a1': nrm((NR, D, LORA_A), D ** -0.5),
        'rw_a2': nrm((NR, LORA_A, D), 0.1 * LORA_A ** -0.5),
        'rw_g1': nrm((NR, D, LORA_G), D ** -0.5),
        'rw_g2': nrm((NR, LORA_G, D), LORA_G ** -0.5),
        'rw_kk': 0.85 + nrm((NR, D), 0.02),
        'rw_ka': 1.0 + nrm((NR, D), 0.02),
        'rw_rk': nrm((NR, RW_H, RW_N), 0.1),
        'rw_wr': nrm((NR, D, D), D ** -0.5),
        'rw_wk': nrm((NR, D, D), D ** -0.5),
        'rw_wv': nrm((NR, D, D), D ** -0.5),
        'rw_wo': nrm((NR, D, D), D ** -0.5 * BETA),
        'rw_gn_g': 1.0 + nrm((NR, D), 0.02),
        'rw_gn_b': nrm((NR, D), 0.02),
        'w_router': nrm((D, N_EXPERTS), D ** -0.5),
        'router_bias': nrm((N_EXPERTS,), 0.01),
        'w_gate': nrm((DEPTH, N_EXPERTS, D, D_FF), D ** -0.5),
        'w_up': nrm((DEPTH, N_EXPERTS, D, D_FF), D ** -0.5),
        'w_down': nrm((DEPTH, N_EXPERTS, D_FF, D), D_FF ** -0.5 * BETA),
        'ln1_g': 1.0 + nrm((DEPTH, D), 0.02),
        'ln1_b': nrm((DEPTH, D), 0.02),
        'ln2_g': 1.0 + nrm((DEPTH, D), 0.02),
        'ln2_b': nrm((DEPTH, D), 0.02),
    }


def reference(x_prompt, x_sample, cache_ka, cache_va, cache_kb, cache_vb, cache_kidx, state_wkv, state_shift,
              page_table, rel_bias, w_in_att, w_out_att, lam_q1, lam_k1, lam_q2, lam_k2, subln_g,
              rw_mix, rw_w0, rw_w1, rw_w2, rw_a0, rw_a1, rw_a2, rw_g1, rw_g2, rw_kk, rw_ka, rw_rk,
              rw_wr, rw_wk, rw_wv, rw_wo, rw_gn_g, rw_gn_b, w_router, router_bias, w_gate, w_up, w_down,
              ln1_g, ln1_b, ln2_g, ln2_b):
    yp, ys = x_prompt, x_sample
    att_rows_p, att_rows_s = [], []
    wkv_p, shift_p, wkv_s, shift_s = [], [], [], []
    for i in range(DEPTH):
        j = i // 2
        if i % 2 == 0:
            lam, lam_init = diff_lambda(lam_q1[j], lam_k1[j], lam_q2[j], lam_k2[j], i)
            mp, rows_p = att_mixer_prompt(yp, w_in_att[j], w_out_att[j], rel_bias, lam, lam_init, subln_g[j])
            ms, rows_s = att_mixer_sample(ys, j, cache_ka, cache_va, cache_kb, cache_vb, cache_kidx, page_table,
                                          w_in_att[j], w_out_att[j], rel_bias, lam, lam_init, subln_g[j])
            att_rows_p.append(rows_p)
            att_rows_s.append(rows_s)
        else:
            rw = (rw_mix[j], rw_w0[j], rw_w1[j], rw_w2[j], rw_a0[j], rw_a1[j], rw_a2[j], rw_g1[j], rw_g2[j],
                  rw_kk[j], rw_ka[j], rw_rk[j], rw_wr[j], rw_wk[j], rw_wv[j], rw_wo[j], rw_gn_g[j], rw_gn_b[j])
            zero_shift = jnp.zeros((yp.shape[0], D_MODEL), yp.dtype)
            zero_wkv = jnp.zeros((yp.shape[0], RW_H, RW_N, RW_N), state_wkv.dtype)
            mp, sp, hp = rwkv7_mix(yp, zero_shift, zero_wkv, *rw)
            ms, ss, hs = rwkv7_mix(ys, state_shift[j], state_wkv[j], *rw)
            wkv_p.append(sp)
            shift_p.append(hp)
            wkv_s.append(ss)
            shift_s.append(hs)
        yp = layer_norm(ALPHA * yp + mp, ln1_g[i], ln1_b[i])
        ys = layer_norm(ALPHA * ys + ms, ln1_g[i], ln1_b[i])
        yp = layer_norm(ALPHA * yp + moe(yp, w_router, router_bias, w_gate[i], w_up[i], w_down[i]), ln2_g[i], ln2_b[i])
        ys = layer_norm(ALPHA * ys + moe(ys, w_router, router_bias, w_gate[i], w_up[i], w_down[i]), ln2_g[i], ln2_b[i])
    new_ka_p, new_va_p, new_kb_p, new_vb_p, new_kidx_p = (jnp.stack(z, 0) for z in zip(*att_rows_p))
    new_ka_s, new_va_s, new_kb_s, new_vb_s, new_kidx_s = (jnp.stack(z, 0) for z in zip(*att_rows_s))
    new_wkv_p = jnp.stack(wkv_p, 0)
    new_shift_p = jnp.stack(shift_p, 0)
    new_wkv_s = jnp.stack(wkv_s, 0)
    new_shift_s = jnp.stack(shift_s, 0)
    return (yp, ys, new_ka_p, new_va_p, new_kb_p, new_vb_p, new_kidx_p,
            new_ka_s, new_va_s, new_kb_s, new_vb_s, new_kidx_s,
            new_wkv_p, new_shift_p, new_wkv_s, new_shift_s)
```

```python
import functools
import math

import numpy as np
import jax
import jax.numpy as jnp
from jax import lax
from jax.experimental import pallas as pl
from jax.experimental.pallas import tpu as pltpu

F32 = jnp.float32
BF16 = jnp.bfloat16
HIGHEST = lax.Precision.HIGHEST

HA, DH = 8, 128
HB, KVB = 16, 2
GB = HB // KVB
H_IDX, D_IDX = 32, 128
TOPK_MAX = 256
IDX_SCALE = (H_IDX * D_IDX) ** -0.5
N_BUCKETS, BUCKET_EXACT, BUCKET_MAX_DIST = 32, 16, 128
RW_N = 64
RW_GN_EPS = 64e-5
N_EXPERTS, N_GROUPS = 16, 4
EXP_PER_GROUP = N_EXPERTS // N_GROUPS
LN_EPS = 1e-5
DEPTH = 2
ALPHA = (2 * DEPTH) ** 0.25
PAGE = 128
ATT_SCALE = DH ** -0.5
MASKED = -1e30

C_QA, C_KA, C_VA = 0, HA * 2 * DH, 2 * HA * 2 * DH
C_QB = 3 * HA * 2 * DH
C_KB = C_QB + HB * DH
C_VB = C_KB + KVB * DH
C_QI = C_VB + KVB * DH
C_KI = C_QI + H_IDX * D_IDX
C_WI = C_KI + D_IDX
IN_WIDTH = C_WI + H_IDX

LANE = 128
VMEM_LIMIT = 56 * 1024 * 1024


def _cparams(sem, vmem=VMEM_LIMIT):
    return pltpu.CompilerParams(dimension_semantics=sem, vmem_limit_bytes=vmem)


def _tile(dim, pref):
    return dim if dim <= pref else pref


def _dot(a, b):
    return jnp.dot(a.astype(BF16), b.astype(BF16), preferred_element_type=F32)


def _dot_nt(a, b):
    return lax.dot_general(a.astype(BF16), b.astype(BF16), (((1,), (1,)), ((), ())),
                           preferred_element_type=F32)


def _mm_body(*refs, nk, n_extra, epilogue, ow):
    x_ref, w_ref = refs[0], refs[1]
    extras = refs[2:2 + n_extra]
    o_ref, acc_ref = refs[2 + n_extra], refs[3 + n_extra]
    k = pl.program_id(2)

    @pl.when(k == 0)
    def _init():
        acc_ref[...] = jnp.zeros_like(acc_ref)

    acc_ref[...] += _dot(x_ref[...], w_ref[...])

    @pl.when(k == nk - 1)
    def _finish():
        acc = acc_ref[...][:, :ow]
        if epilogue is not None:
            acc = epilogue(acc, *[e[...] for e in extras])
        o_ref[...] = acc.astype(o_ref.dtype)


def _matmul(x, w, layer, *, col0=0, ncols=None, out_dtype=F32, epilogue=None, extras=(),
            tm=1024, tn=1024, tk=512):
    m, kdim = x.shape
    nfull = w.shape[-1]
    ncols = nfull if ncols is None else ncols
    tm = _tile(m, tm)
    tk = _tile(kdim, tk)
    assert m % tm == 0 and kdim % tk == 0
    if ncols <= tn:
        if col0 == 0 and ncols == nfull:
            tnw = ncols
        else:
            tnw = -(-ncols // LANE) * LANE
        tno = ncols
    else:
        tnw = tno = tn
    assert col0 % tnw == 0
    cb = col0 // tnw
    nk = kdim // tk
    grid = (m // tm, pl.cdiv(ncols, tno), nk)
    in_specs = [pl.BlockSpec((tm, tk), lambda i, j, k: (i, k)),
                pl.BlockSpec((None, tk, tnw), lambda i, j, k: (layer, k, j + cb))]
    for _ in extras:
        in_specs.append(pl.BlockSpec((1, tno), lambda i, j, k: (0, j)))
    return pl.pallas_call(
        functools.partial(_mm_body, nk=nk, n_extra=len(extras), epilogue=epilogue, ow=tno),
        grid=grid,
        in_specs=in_specs,
        out_specs=pl.BlockSpec((tm, tno), lambda i, j, k: (i, j)),
        out_shape=jax.ShapeDtypeStruct((m, ncols), out_dtype),
        scratch_shapes=[pltpu.VMEM((tm, tnw), F32)],
        compiler_params=_cparams(("parallel", "parallel", "arbitrary")),
    )(x, w, *extras)


def _ln_body(x_ref, m_ref, g_ref, b_ref, o_ref):
    z = ALPHA * x_ref[...] + m_ref[...]
    mu = jnp.mean(z, -1, keepdims=True)
    zc = z - mu
    var = jnp.mean(zc * zc, -1, keepdims=True)
    o_ref[...] = zc * lax.rsqrt(var + LN_EPS) * g_ref[...] + b_ref[...]


def _ln_res(x, m, g, b):
    n, d = x.shape
    tm = _tile(n, 256)
    row = pl.BlockSpec((tm, d), lambda i: (i, 0))
    vec = pl.BlockSpec((1, d), lambda i: (0, 0))
    return pl.pallas_call(
        _ln_body, grid=(n // tm,), in_specs=[row, row, vec, vec], out_specs=row,
        out_shape=jax.ShapeDtypeStruct((n, d), F32),
        compiler_params=_cparams(("parallel",)),
    )(x, m, g[None], b[None])


def _t5_bucket_np(dist):
    n = np.maximum(dist, 0)
    nf = np.maximum(n, 1).astype(np.float32)
    large = BUCKET_EXACT + (np.log(nf / np.float32(BUCKET_EXACT)) / np.float32(math.log(BUCKET_MAX_DIST / BUCKET_EXACT))
                            * np.float32(N_BUCKETS - BUCKET_EXACT)).astype(np.int32)
    large = np.minimum(large, N_BUCKETS - 1)
    return np.where(n < BUCKET_EXACT, n, large).astype(np.int32)


def _bias_body(rel_ref, bkt_ref, o_ref):
    h = pl.program_id(0)
    bkt = bkt_ref[...]
    out = jnp.zeros(bkt.shape, F32)
    for b in range(N_BUCKETS):
        out = jnp.where(bkt == b, rel_ref[b, h], out)
    o_ref[0] = out


def _bias_tiles(rel_bias, buckets):
    nh = rel_bias.shape[1]
    n, r, c = buckets.shape
    return pl.pallas_call(
        _bias_body, grid=(nh,),
        in_specs=[pl.BlockSpec(memory_space=pltpu.SMEM),
                  pl.BlockSpec((n, r, c), lambda h: (0, 0, 0))],
        out_specs=pl.BlockSpec((1, n, r, c), lambda h: (h, 0, 0, 0)),
        out_shape=jax.ShapeDtypeStruct((nh, n, r, c), F32),
        compiler_params=_cparams(("arbitrary",)),
    )(rel_bias, jnp.asarray(buckets))


def _softmax_step(s, valid, m_old, l_old):
    s = jnp.where(valid, s, MASKED)
    m_new = jnp.maximum(m_old, jnp.max(s, -1, keepdims=True))
    p = jnp.where(valid, jnp.exp(s - m_new), 0.0)
    alpha = jnp.exp(m_old - m_new)
    l_new = alpha * l_old + jnp.sum(p, -1, keepdims=True)
    return p, alpha, m_new, l_new


def _diff_finish(o1, o2, lv, g, lam_init):
    lam = (jnp.exp(jnp.sum(lv[0:1] * lv[1:2], -1, keepdims=True))
           - jnp.exp(jnp.sum(lv[2:3] * lv[3:4], -1, keepdims=True)) + lam_init)
    o = o1 - lam * o2
    o = o * lax.rsqrt(jnp.mean(o * o, -1, keepdims=True) + 1e-5)
    return o * g * (1.0 - lam_init)


def _pdiff_body(rel_ref, lam_ref, q_ref, k_ref, v_ref, bias_ref, g_ref, o_ref, m_ref, l_ref, acc_ref,
                *, t, nk, lam_init):
    h, iq, ik = pl.program_id(1), pl.program_id(2), pl.program_id(3)

    @pl.when(ik == 0)
    def _init():
        m_ref[...] = jnp.full(m_ref.shape, MASKED, F32)
        l_ref[...] = jnp.zeros_like(l_ref)
        acc_ref[...] = jnp.zeros_like(acc_ref)

    @pl.when(ik <= iq)
    def _step():
        q = q_ref[0]
        k = k_ref[0]
        v = v_ref[0].astype(BF16)
        bias = jnp.where(ik >= iq - 1, bias_ref[0, 0], rel_ref[N_BUCKETS - 1, h])
        row = lax.broadcasted_iota(jnp.int32, (t, t), 0)
        col = lax.broadcasted_iota(jnp.int32, (t, t), 1)
        valid = jnp.logical_or(ik < iq, col <= row)
        for c in range(2):
            s = _dot_nt(q[:, c * DH:(c + 1) * DH], k[:, c * DH:(c + 1) * DH]) * ATT_SCALE + bias
            p, alpha, m_new, l_new = _softmax_step(s, valid, m_ref[c], l_ref[c])
            acc_ref[c] = alpha * acc_ref[c] + jnp.dot(p.astype(BF16), v, preferred_element_type=F32)
            m_ref[c] = m_new
            l_ref[c] = l_new

    @pl.when(ik == nk - 1)
    def _finish():
        o = _diff_finish(acc_ref[0] / l_ref[0], acc_ref[1] / l_ref[1], lam_ref[...], g_ref[...], lam_init)
        o_ref[0] = o.astype(o_ref.dtype)


def _prompt_diff(qa, ka, va, bias_a, rel_bias, lam, subln_g, lam_init, t):
    b, s, _ = qa.shape
    nq = s // t
    wv = 2 * DH
    return pl.pallas_call(
        functools.partial(_pdiff_body, t=t, nk=nq, lam_init=lam_init),
        grid=(b, HA, nq, nq),
        in_specs=[pl.BlockSpec(memory_space=pltpu.SMEM),
                  pl.BlockSpec((4, DH), lambda b_, h, iq, ik: (0, 0)),
                  pl.BlockSpec((1, t, wv), lambda b_, h, iq, ik: (b_, iq, h)),
                  pl.BlockSpec((1, t, wv), lambda b_, h, iq, ik: (b_, jnp.minimum(ik, iq), h)),
                  pl.BlockSpec((1, t, wv), lambda b_, h, iq, ik: (b_, jnp.minimum(ik, iq), h)),
                  pl.BlockSpec((1, 1, t, t), lambda b_, h, iq, ik: (h, jnp.where(ik >= iq, 0, 1), 0, 0)),
                  pl.BlockSpec((1, wv), lambda b_, h, iq, ik: (0, 0))],
        out_specs=pl.BlockSpec((1, t, wv), lambda b_, h, iq, ik: (b_, iq, h)),
        out_shape=jax.ShapeDtypeStruct((b, s, HA * wv), BF16),
        scratch_shapes=[pltpu.VMEM((2, t, 1), F32), pltpu.VMEM((2, t, 1), F32), pltpu.VMEM((2, t, wv), F32)],
        compiler_params=_cparams(("parallel", "parallel", "parallel", "arbitrary")),
    )(rel_bias, lam, qa, ka, va, bias_a, subln_g)


def _count(cond):
    return jnp.sum(jnp.where(cond, 1.0, 0.0), -1, keepdims=True)


def _topk_mask(sc, k, n_idx_bits):
    bits = lax.bitcast_convert_type(sc, jnp.int32)
    key = bits ^ ((bits >> 31) & jnp.int32(0x7FFFFFFF))
    kf = jnp.float32(k)
    int_min = jnp.int32(-2 ** 31)
    thr = jnp.where(_count(key >= 0) >= kf, jnp.int32(0), int_min)
    for bit in range(30, -1, -1):
        cand = thr | jnp.int32(1 << bit)
        thr = jnp.where(_count(key >= cand) >= kf, cand, thr)
    above = key > thr
    tie = key == thr
    need = kf - _count(above)
    idx = lax.broadcasted_iota(jnp.int32, sc.shape, sc.ndim - 1)
    last = jnp.zeros_like(thr)
    for bit in range(n_idx_bits - 1, -1, -1):
        cand = last | jnp.int32(1 << bit)
        last = jnp.where(_count(jnp.logical_and(tie, idx < cand)) < need, cand, last)
    return jnp.logical_or(above, jnp.logical_and(tie, idx <= last))


def _pindex_body(qi_ref, ki_ref, wi_ref, o_ref, *, tq, s, topk, nbits):
    iq = pl.program_id(1)
    ki = ki_ref[0].astype(BF16)
    wi = wi_ref[0]
    sc = jnp.zeros((tq, s), F32)
    for h in range(H_IDX):
        d = lax.dot_general(qi_ref[0, :, h * D_IDX:(h + 1) * D_IDX], ki, (((1,), (1,)), ((), ())),
                            preferred_element_type=F32)
        sc = sc + jnp.maximum(d, 0.0) * wi[:, h:h + 1]
    sc = sc * IDX_SCALE
    qpos = iq * tq + lax.broadcasted_iota(jnp.int32, (tq, s), 0)
    kpos = lax.broadcasted_iota(jnp.int32, (tq, s), 1)
    causal = kpos <= qpos
    sc = jnp.where(causal, sc, -jnp.inf)
    sel = jnp.logical_and(_topk_mask(sc, topk, nbits), causal)
    o_ref[0] = jnp.where(sel, 1.0, 0.0)


def _prompt_select(qi, ki, wi, topk):
    b, s, _ = qi.shape
    tq = _tile(s, 128)
    nbits = int(s).bit_length()
    return pl.pallas_call(
        functools.partial(_pindex_body, tq=tq, s=s, topk=topk, nbits=nbits),
        grid=(b, s // tq),
        in_specs=[pl.BlockSpec((1, tq, H_IDX * D_IDX), lambda b_, i: (b_, i, 0)),
                  pl.BlockSpec((1, s, D_IDX), lambda b_, i: (b_, 0, 0)),
                  pl.BlockSpec((1, tq, H_IDX), lambda b_, i: (b_, i, 0))],
        out_specs=pl.BlockSpec((1, tq, s), lambda b_, i: (b_, i, 0)),
        out_shape=jax.ShapeDtypeStruct((b, s, s), F32),
        compiler_params=_cparams(("parallel", "parallel")),
    )(qi, ki, wi)


def _psparse_body(rel_ref, q_ref, k_ref, v_ref, msk_ref, bias_ref, o_ref, m_ref, l_ref, acc_ref, *, t, nk):
    g, iq, ik = pl.program_id(1), pl.program_id(2), pl.program_id(3)

    @pl.when(ik == 0)
    def _init():
        m_ref[...] = jnp.full(m_ref.shape, MASKED, F32)
        l_ref[...] = jnp.zeros_like(l_ref)
        acc_ref[...] = jnp.zeros_like(acc_ref)

    @pl.when(ik <= iq)
    def _step():
        k = k_ref[0].astype(BF16)
        v = v_ref[0].astype(BF16)
        valid = msk_ref[0] > 0.5
        near = ik >= iq - 1
        for r in range(GB):
            bias = jnp.where(near, bias_ref[r, 0], rel_ref[N_BUCKETS - 1, HA + g * GB + r])
            s = _dot_nt(q_ref[0, :, r * DH:(r + 1) * DH], k) * ATT_SCALE + bias
            p, alpha, m_new, l_new = _softmax_step(s, valid, m_ref[r], l_ref[r])
            acc_ref[r] = alpha * acc_ref[r] + jnp.dot(p.astype(BF16), v, preferred_element_type=F32)
            m_ref[r] = m_new
            l_ref[r] = l_new

    @pl.when(ik == nk - 1)
    def _finish():
        for r in range(GB):
            o_ref[0, :, r * DH:(r + 1) * DH] = (acc_ref[r] / l_ref[r]).astype(o_ref.dtype)


def _prompt_sparse(qb, kb, vb, mask, bias_b, rel_bias, t):
    b, s, _ = qb.shape
    nq = s // t
    return pl.pallas_call(
        functools.partial(_psparse_body, t=t, nk=nq),
        grid=(b, KVB, nq, nq),
        in_specs=[pl.BlockSpec(memory_space=pltpu.SMEM),
                  pl.BlockSpec((1, t, GB * DH), lambda b_, g, iq, ik: (b_, iq, g)),
                  pl.BlockSpec((1, t, DH), lambda b_, g, iq, ik: (b_, jnp.minimum(ik, iq), g)),
                  pl.BlockSpec((1, t, DH), lambda b_, g, iq, ik: (b_, jnp.minimum(ik, iq), g)),
                  pl.BlockSpec((1, t, t), lambda b_, g, iq, ik: (b_, iq, jnp.minimum(ik, iq))),
                  pl.BlockSpec((GB, 1, t, t), lambda b_, g, iq, ik: (g, jnp.where(ik >= iq, 0, 1), 0, 0))],
        out_specs=pl.BlockSpec((1, t, GB * DH), lambda b_, g, iq, ik: (b_, iq, g)),
        out_shape=jax.ShapeDtypeStruct((b, s, HB * DH), BF16),
        scratch_shapes=[pltpu.VMEM((GB, t, 1), F32), pltpu.VMEM((GB, t, 1), F32), pltpu.VMEM((GB, t, DH), F32)],
        compiler_params=_cparams(("parallel", "parallel", "parallel", "arbitrary")),
    )(rel_bias, qb, kb, vb, mask, bias_b)


def _sindex_body(pt_ref, q_ref, w_ref, kc_ref, kn_ref, o_ref, *, n_pages):
    p = pl.program_id(1)
    keys = jnp.where(p < n_pages, kc_ref[0, 0], kn_ref[0])
    d = _dot_nt(q_ref[0], keys)
    r = jnp.maximum(d, 0.0) * w_ref[0]
    o_ref[0] = jnp.sum(r.reshape(H_IDX, 8, PAGE), 0) * IDX_SCALE


def _sample_index(page_table, qi_rows, wi_col, cache_kidx, ki_new, layer):
    bd, n_pages = page_table.shape
    rows = qi_rows.shape[1]
    grid_spec = pltpu.PrefetchScalarGridSpec(
        num_scalar_prefetch=1, grid=(bd, n_pages + 1),
        in_specs=[pl.BlockSpec((1, rows, D_IDX), lambda b, p, pt: (b, 0, 0)),
                  pl.BlockSpec((1, rows, 1), lambda b, p, pt: (b, 0, 0)),
                  pl.BlockSpec((1, 1, PAGE, D_IDX),
                               lambda b, p, pt: (layer, pt[b, jnp.minimum(p, n_pages - 1)], 0, 0)),
                  pl.BlockSpec((1, PAGE, D_IDX), lambda b, p, pt: (b, 0, 0))],
        out_specs=pl.BlockSpec((1, 8, PAGE), lambda b, p, pt: (b, 0, p)))
    return pl.pallas_call(
        functools.partial(_sindex_body, n_pages=n_pages), grid_spec=grid_spec,
        out_shape=jax.ShapeDtypeStruct((bd, 8, (n_pages + 1) * PAGE), F32),
        compiler_params=_cparams(("parallel", "arbitrary")),
    )(page_table, qi_rows, wi_col, cache_kidx, ki_new)


def _sselect_body(sc_ref, o_ref, *, past_len, topk, nbits):
    sc = sc_ref[0]
    qpos = past_len + lax.broadcasted_iota(jnp.int32, sc.shape, 0)
    kpos = lax.broadcasted_iota(jnp.int32, sc.shape, 1)
    causal = kpos <= qpos
    sc = jnp.where(causal, sc, -jnp.inf)
    sel = jnp.logical_and(_topk_mask(sc, topk, nbits), causal)
    o_ref[0] = jnp.where(sel, 1.0, 0.0)


def _sample_select(sc, past_len, topk):
    bd, t, l = sc.shape
    spec = pl.BlockSpec((1, t, l), lambda b: (b, 0, 0))
    return pl.pallas_call(
        functools.partial(_sselect_body, past_len=past_len, topk=topk, nbits=int(l).bit_length()),
        grid=(bd,), in_specs=[spec], out_specs=spec,
        out_shape=jax.ShapeDtypeStruct((bd, t, l), F32),
        compiler_params=_cparams(("parallel",)),
    )(sc)


def _sattn_body(*refs, n_pages, diff, lam_init):
    if diff:
        (pt_ref, lam_ref, q_ref, kc_ref, vc_ref, kn_ref, vn_ref, far_ref, near_ref, g_ref,
         o_ref, m_ref, l_ref, acc_ref) = refs
    else:
        (pt_ref, q_ref, kc_ref, vc_ref, kn_ref, vn_ref, far_ref, near_ref, msk_ref,
         o_ref, m_ref, l_ref, acc_ref) = refs
    p = pl.program_id(1)

    @pl.when(p == 0)
    def _init():
        m_ref[...] = jnp.full(m_ref.shape, MASKED, F32)
        l_ref[...] = jnp.zeros_like(l_ref)
        acc_ref[...] = jnp.zeros_like(acc_ref)

    from_cache = p < n_pages
    k = jnp.where(from_cache, kc_ref[0, 0], kn_ref[0]).astype(BF16)
    v = jnp.where(from_cache, vc_ref[0, 0], vn_ref[0]).astype(BF16)
    q = q_ref[0]
    rows = q.shape[0]
    tok = lax.broadcasted_iota(jnp.int32, (rows, PAGE), 0) % 8
    col = lax.broadcasted_iota(jnp.int32, (rows, PAGE), 1)
    valid = jnp.logical_or(from_cache, col <= tok)
    if diff:
        s = _dot_nt(q, k)
    else:
        half = rows // KVB
        valid = jnp.logical_and(valid, jnp.concatenate([msk_ref[0]] * (rows // 8), 0) > 0.5)
        s = jnp.concatenate([_dot_nt(q[g * half:(g + 1) * half], k[:, g * DH:(g + 1) * DH])
                             for g in range(KVB)], 0)
    bias = jnp.where(p < n_pages - 1, far_ref[...], near_ref[0])
    s = s * ATT_SCALE + bias
    pr, alpha, m_new, l_new = _softmax_step(s, valid, m_ref[...], l_ref[...])
    pr = pr.astype(BF16)
    if diff:
        pv = jnp.dot(pr, v, preferred_element_type=F32)
    else:
        pv = jnp.concatenate([jnp.dot(pr[g * half:(g + 1) * half], v[:, g * DH:(g + 1) * DH],
                                      preferred_element_type=F32) for g in range(KVB)], 0)
    acc_ref[...] = alpha * acc_ref[...] + pv
    m_ref[...] = m_new
    l_ref[...] = l_new

    @pl.when(p == n_pages)
    def _finish():
        o = acc_ref[...] / l_ref[...]
        if diff:
            wv = 2 * DH
            for h in range(HA):
                o1 = o[h * 16:h * 16 + 8, h * wv:(h + 1) * wv]
                o2 = o[h * 16 + 8:h * 16 + 16, h * wv:(h + 1) * wv]
                o_ref[0, :, h * wv:(h + 1) * wv] = _diff_finish(o1, o2, lam_ref[...], g_ref[...], lam_init)
        else:
            o_ref[0] = o


def _sample_attn(page_table, q_rows, kcache, vcache, knew, vnew, far_col, near, layer, *,
                 diff, lam=None, subln_g=None, lam_init=0.0, mask=None):
    bd, n_pages = page_table.shape
    rows, qw = q_rows.shape[1:]
    wk, wv = kcache.shape[-1], vcache.shape[-1]

    def page(b, p, pt):
        return (layer, pt[b, jnp.minimum(p, n_pages - 1)], 0, 0)

    def per_b(b, p, pt):
        return (b, 0, 0)

    in_specs = [pl.BlockSpec((1, rows, qw), per_b),
                pl.BlockSpec((1, 1, PAGE, wk), page),
                pl.BlockSpec((1, 1, PAGE, wv), page),
                pl.BlockSpec((1, PAGE, wk), per_b),
                pl.BlockSpec((1, PAGE, wv), per_b),
                pl.BlockSpec((rows, 1), lambda b, p, pt: (0, 0)),
                pl.BlockSpec((1, rows, PAGE), lambda b, p, pt: (jnp.clip(p - (n_pages - 1), 0, 1), 0, 0))]
    args = [q_rows, kcache, vcache, knew, vnew, far_col, near]
    if diff:
        in_specs = ([pl.BlockSpec((4, DH), lambda b, p, pt: (0, 0))] + in_specs
                    + [pl.BlockSpec((1, wv // HA), lambda b, p, pt: (0, 0))])
        args = [lam] + args + [subln_g]
        out_spec = pl.BlockSpec((1, 8, wv), per_b)
        out_shape = jax.ShapeDtypeStruct((bd, 8, wv), F32)
        acc_w = wv
    else:
        in_specs = in_specs + [pl.BlockSpec((1, 8, PAGE), lambda b, p, pt: (b, 0, p))]
        args = args + [mask]
        out_spec = pl.BlockSpec((1, rows, DH), per_b)
        out_shape = jax.ShapeDtypeStruct((bd, rows, DH), F32)
        acc_w = DH
    grid_spec = pltpu.PrefetchScalarGridSpec(
        num_scalar_prefetch=1, grid=(bd, n_pages + 1), in_specs=in_specs, out_specs=out_spec,
        scratch_shapes=[pltpu.VMEM((rows, 1), F32), pltpu.VMEM((rows, 1), F32), pltpu.VMEM((rows, acc_w), F32)])
    return pl.pallas_call(
        functools.partial(_sattn_body, n_pages=n_pages, diff=diff, lam_init=lam_init),
        grid_spec=grid_spec, out_shape=out_shape,
        compiler_params=_cparams(("parallel", "arbitrary")),
    )(page_table, *args)


def _rwprep_body(x_ref, p8_ref, sh_ref, mix_ref, *o_refs):
    it = pl.program_id(1)
    x = x_ref[0]
    prev_last = jnp.where(it == 0, sh_ref[0], p8_ref[0, 7:8, :])
    row = lax.broadcasted_iota(jnp.int32, x.shape, 0)
    x_prev = jnp.where(row == 0, prev_last, pltpu.roll(x, 1, 0))
    xx = x_prev - x
    for c, o_ref in enumerate(o_refs):
        o_ref[0] = (x + xx * mix_ref[c:c + 1, :]).astype(o_ref.dtype)


def _rw_prep(x, shift0, mix):
    b, t, d = x.shape
    tt = _tile(t, 512)
    td = _tile(d, 1024)
    blk = pl.BlockSpec((1, tt, td), lambda b_, i, j: (b_, i, j))
    return pl.pallas_call(
        _rwprep_body, grid=(b, t // tt, d // td),
        in_specs=[blk,
                  pl.BlockSpec((1, 8, td), lambda b_, i, j: (b_, jnp.maximum(i * (tt // 8) - 1, 0), j)),
                  pl.BlockSpec((1, 1, td), lambda b_, i, j: (b_, 0, j)),
                  pl.BlockSpec((6, td), lambda b_, i, j: (0, j))],
        out_specs=[blk] * 6,
        out_shape=[jax.ShapeDtypeStruct((b, t, d), BF16)] * 6,
        compiler_params=_cparams(("parallel", "parallel", "parallel")),
    )(x, x, shift0[:, None], mix)


def _scan_body(r_ref, w_ref, k_ref, v_ref, a_ref, g_ref, kk_ref, ka_ref, rk_ref, gg_ref, gb_ref, s0_ref,
               y_ref, sf_ref, st_ref, *, heads, c, nchunks, exact):
    ic = pl.program_id(2)

    @pl.when(ic == 0)
    def _init():
        st_ref[...] = s0_ref[0]

    if exact:
        def mm(a, b, dims=(((1,), (0,)), ((), ()))):
            return lax.dot_general(a, b, dims, precision=HIGHEST, preferred_element_type=F32)
    else:
        def mm(a, b, dims=(((1,), (0,)), ((), ()))):
            return lax.dot_general(a.astype(BF16), b.astype(BF16), dims, preferred_element_type=F32)
    nt = (((1,), (1,)), ((), ()))
    tn = (((0,), (0,)), ((), ()))

    row = lax.broadcasted_iota(jnp.int32, (c, c), 0)
    col = lax.broadcasted_iota(jnp.int32, (c, c), 1)
    strict = row > col
    incl = row >= col
    lw_all = -jnp.exp(w_ref[0])
    cum_all = jnp.dot(jnp.where(incl, 1.0, 0.0), lw_all, precision=HIGHEST, preferred_element_type=F32)
    ys = []
    for h in range(heads):
        sl = slice(h * RW_N, (h + 1) * RW_N)
        r, kraw, v, a = r_ref[0, :, sl], k_ref[0, :, sl], v_ref[0, :, sl], a_ref[0, :, sl]
        lw, cum = lw_all[:, sl], cum_all[:, sl]
        kk = kraw * kk_ref[:, sl]
        kk = kk * lax.rsqrt(jnp.maximum(jnp.sum(kk * kk, -1, keepdims=True), 1e-24))
        kmod = kraw * (1.0 + (a - 1.0) * ka_ref[:, sl])
        bb = kk * a
        e_neg = jnp.exp(-cum)
        left = jnp.concatenate([kk * jnp.exp(cum - lw), r * jnp.exp(cum)], 0)
        right = jnp.concatenate([bb * e_neg, kmod * e_neg], 0)
        s0 = st_ref[h]
        amat = mm(left, right, nt)
        ps = mm(left, s0, nt)
        m_bk = jnp.where(strict, amat[:c, :c], 0.0)
        m_kk = jnp.where(strict, amat[:c, c:], 0.0)
        n_rb = jnp.where(incl, amat[c:, :c], 0.0)
        n_rk = jnp.where(incl, amat[c:, c:], 0.0)
        x = -(ps[:c] + mm(m_kk, v))
        mp = m_bk
        x = x - mm(mp, x)
        pw = 1
        while 2 * pw < c:
            mp = mm(mp, mp)
            pw *= 2
            x = x + mm(mp, x)
        o = ps[c:] + mm(n_rb, x) + mm(n_rk, v)
        cum_end = cum[c - 1:c, :]
        e_end = jnp.exp(cum_end - cum)
        st_ref[h] = s0 * jnp.exp(cum_end) + mm(x, bb * e_end, tn) + mm(v, kmod * e_end, tn)
        mu = jnp.mean(o, -1, keepdims=True)
        oc = o - mu
        var = jnp.mean(oc * oc, -1, keepdims=True)
        on = oc * lax.rsqrt(var + RW_GN_EPS) * gg_ref[:, sl] + gb_ref[:, sl]
        bonus = jnp.sum(r * kmod * rk_ref[:, sl], -1, keepdims=True) * v
        ys.append((on + bonus) * g_ref[0, :, sl])
    y_ref[0] = jnp.concatenate(ys, -1).astype(y_ref.dtype)

    @pl.when(ic == nchunks - 1)
    def _finish():
        sf_ref[0] = st_ref[...]


def _rw_scan(r, w, k, v, a, g, k_k, k_a, r_k, gn_g, gn_b, state0, *, chunk, heads, exact):
    b, t, d = r.shape
    nh = d // RW_N
    c = _tile(t, chunk)
    heads = min(heads, nh)
    lanes = heads * RW_N
    seq = pl.BlockSpec((1, c, lanes), lambda b_, hg, ic: (b_, ic, hg))
    vec = pl.BlockSpec((1, lanes), lambda b_, hg, ic: (0, hg))
    st = pl.BlockSpec((1, heads, RW_N, RW_N), lambda b_, hg, ic: (b_, hg, 0, 0))
    return pl.pallas_call(
        functools.partial(_scan_body, heads=heads, c=c, nchunks=t // c, exact=exact),
        grid=(b, nh // heads, t // c),
        in_specs=[seq] * 6 + [vec] * 5 + [st],
        out_specs=[seq, st],
        out_shape=[jax.ShapeDtypeStruct((b, t, d), BF16), jax.ShapeDtypeStruct((b, nh, RW_N, RW_N), F32)],
        scratch_shapes=[pltpu.VMEM((heads, RW_N, RW_N), F32)],
        compiler_params=_cparams(("parallel", "parallel", "arbitrary")),
    )(r, w, k, v, a, g, k_k[None], k_a[None], r_k.reshape(1, d), gn_g[None], gn_b[None], state0)


def _router_body(x_ref, w_ref, b_ref, o_ref):
    logits = jnp.dot(x_ref[...], w_ref[...], precision=HIGHEST, preferred_element_type=F32)
    scores = jax.nn.sigmoid(logits)
    sel = scores + b_ref[...]
    sc = [sel[:, e:e + 1] for e in range(N_EXPERTS)]
    raw = [scores[:, e:e + 1] for e in range(N_EXPERTS)]
    best = jnp.zeros_like(sc[0]).astype(jnp.int32)
    best_score = None
    for gi in range(N_GROUPS):
        m = sc[gi * EXP_PER_GROUP:(gi + 1) * EXP_PER_GROUP]
        pair = None
        for i in range(EXP_PER_GROUP):
            for j in range(i + 1, EXP_PER_GROUP):
                pair = m[i] + m[j] if pair is None else jnp.maximum(pair, m[i] + m[j])
        if best_score is None:
            best_score = pair
        else:
            upd = pair > best_score
            best = jnp.where(upd, gi, best)
            best_score = jnp.where(upd, pair, best_score)

    def pick(vals, i):
        out = vals[i]
        for gi in range(1, N_GROUPS):
            out = jnp.where(best == gi, vals[gi * EXP_PER_GROUP + i], out)
        return out

    a = [pick(sc, i) for i in range(EXP_PER_GROUP)]
    cw = [pick(raw, i) for i in range(EXP_PER_GROUP)]
    i1, v1, w1 = jnp.zeros_like(best), a[0], cw[0]
    for i in range(1, EXP_PER_GROUP):
        upd = a[i] > v1
        i1, v1, w1 = jnp.where(upd, i, i1), jnp.where(upd, a[i], v1), jnp.where(upd, cw[i], w1)
    i2, v2, w2 = None, None, None
    for i in range(EXP_PER_GROUP):
        ok = i1 != i
        if v2 is None:
            i2, v2, w2 = jnp.full_like(best, i), jnp.where(ok, a[i], -jnp.inf), cw[i]
        else:
            upd = jnp.logical_and(ok, a[i] > v2)
            i2, v2, w2 = jnp.where(upd, i, i2), jnp.where(upd, a[i], v2), jnp.where(upd, cw[i], w2)
    tot = w1 + w2
    e1 = best * EXP_PER_GROUP + i1
    e2 = best * EXP_PER_GROUP + i2
    lane = lax.broadcasted_iota(jnp.int32, scores.shape, 1)
    o_ref[...] = jnp.where(lane == e1, w1 / tot, 0.0) + jnp.where(lane == e2, w2 / tot, 0.0)


def _router(x, w_router, router_bias):
    n, d = x.shape
    tm = _tile(n, 512)
    return pl.pallas_call(
        _router_body, grid=(n // tm,),
        in_specs=[pl.BlockSpec((tm, d), lambda i: (i, 0)),
                  pl.BlockSpec((d, LANE), lambda i: (0, 0)),
                  pl.BlockSpec((1, LANE), lambda i: (0, 0))],
        out_specs=pl.BlockSpec((tm, LANE), lambda i: (i, 0)),
        out_shape=jax.ShapeDtypeStruct((n, LANE), F32),
        compiler_params=_cparams(("parallel",)),
    )(x, jnp.pad(w_router, ((0, 0), (0, LANE - N_EXPERTS))),
      jnp.pad(router_bias, (0, LANE - N_EXPERTS))[None])


def _experts_body(x_ref, gt_ref, wg_ref, wu_ref, wd_ref, o_ref):
    e, f = pl.program_id(1), pl.program_id(2)

    @pl.when(jnp.logical_and(e == 0, f == 0))
    def _init():
        o_ref[...] = jnp.zeros_like(o_ref)

    x = x_ref[...]
    gates = gt_ref[...]
    lane = lax.broadcasted_iota(jnp.int32, gates.shape, 1)
    gate = jnp.sum(jnp.where(lane == e, gates, 0.0), -1, keepdims=True)
    hg = _dot(x, wg_ref[0, 0])
    hu = _dot(x, wu_ref[0, 0])
    hid = hg * jax.nn.sigmoid(hg) * hu * gate
    o_ref[...] += _dot(hid, wd_ref[0, 0])


def _experts(x, gates, w_gate, w_up, w_down, layer):
    n, d = x.shape
    ff = w_gate.shape[-1]
    tm = _tile(n, 512)
    tf = _tile(ff, 256)
    return pl.pallas_call(
        _experts_body, grid=(n // tm, N_EXPERTS, ff // tf),
        in_specs=[pl.BlockSpec((tm, d), lambda i, e, f: (i, 0)),
                  pl.BlockSpec((tm, LANE), lambda i, e, f: (i, 0)),
                  pl.BlockSpec((1, 1, d, tf), lambda i, e, f: (layer, e, 0, f)),
                  pl.BlockSpec((1, 1, d, tf), lambda i, e, f: (layer, e, 0, f)),
                  pl.BlockSpec((1, 1, tf, d), lambda i, e, f: (layer, e, f, 0))],
        out_specs=pl.BlockSpec((tm, d), lambda i, e, f: (i, 0)),
        out_shape=jax.ShapeDtypeStruct((n, d), F32),
        compiler_params=_cparams(("parallel", "arbitrary", "arbitrary")),
    )(x, gates, w_gate, w_up, w_down)


def _moe(x, w_router, router_bias, w_gate, w_up, w_down, layer):
    gates = _router(x, w_router, router_bias)
    return _experts(x.astype(BF16), gates, w_gate, w_up, w_down, layer)


def _project_att(x2, w_in, j):
    mm = functools.partial(_matmul, x2, w_in, j)
    wa = HA * 2 * DH
    qa = mm(col0=C_QA, ncols=wa, out_dtype=BF16)
    ka = mm(col0=C_KA, ncols=wa)
    va = mm(col0=C_VA, ncols=wa)
    qb = mm(col0=C_QB, ncols=HB * DH, out_dtype=BF16)
    kb = mm(col0=C_KB, ncols=KVB * DH)
    vb = mm(col0=C_VB, ncols=KVB * DH)
    qi = mm(col0=C_QI, ncols=H_IDX * D_IDX, out_dtype=BF16, tn=512)
    kw = mm(col0=C_KI, ncols=D_IDX + H_IDX)
    return qa, ka, va, qb, kb, vb, qi, kw[:, :D_IDX], kw[:, D_IDX:]


def _att_prompt(x, w_in, w_out, j, rel_bias, lam, lam_init, subln_g):
    b, s, d = x.shape
    x2 = x.reshape(b * s, d).astype(BF16)
    qa, ka, va, qb, kb, vb, qi, ki, wi = _project_att(x2, w_in, j)
    t = _tile(s, 256)
    r = np.arange(t)
    dist = r[:, None] - r[None, :]
    tiles = _bias_tiles(rel_bias, np.stack([_t5_bucket_np(dist), _t5_bucket_np(dist + t)]))
    sh = lambda z: z.reshape(b, s, z.shape[-1])
    oa = _prompt_diff(sh(qa), sh(ka), sh(va), tiles[:HA], rel_bias, lam, subln_g, lam_init, t)
    mask = _prompt_select(sh(qi), sh(ki), sh(wi), min(TOPK_MAX, s // 4))
    ob = _prompt_sparse(sh(qb), sh(kb), sh(vb), mask, tiles[HA:], rel_bias, t)
    o = jnp.concatenate([oa, ob], -1).reshape(b * s, -1)
    y = _matmul(o, w_out, j)
    return y, (ka, va, kb, vb, ki)


def _att_sample(x, j, cache_ka, cache_va, cache_kb, cache_vb, cache_kidx, page_table,
                w_in, w_out, rel_bias, lam, lam_init, subln_g):
    bd, t, d = x.shape
    assert t == 8
    n_pages = page_table.shape[1]
    past_len = n_pages * PAGE
    na, n_pool = cache_ka.shape[:2]
    x2 = x.reshape(bd * t, d).astype(BF16)
    qa, ka, va, qb, kb, vb, qi, ki, wi = _project_att(x2, w_in, j)

    def pad_rows(z):
        return jnp.pad(z.reshape(bd, t, z.shape[-1]), ((0, 0), (0, PAGE - t), (0, 0)))

    tt = np.arange(t)[:, None]
    cc = np.arange(PAGE)[None, :]
    tiles = _bias_tiles(rel_bias, np.stack([_t5_bucket_np(PAGE + tt - cc), _t5_bucket_np(tt - cc)]))
    far = rel_bias[N_BUCKETS - 1]

    nmap = HA * 2
    qa5 = qa.reshape(bd, t, nmap, DH).transpose(0, 2, 1, 3)
    q_bd = (qa5[:, :, :, None, :] * jnp.eye(nmap, dtype=BF16)[None, :, None, :, None]
            ).reshape(bd, nmap * t, nmap * DH)
    near_a = jnp.broadcast_to(tiles[:HA].transpose(1, 0, 2, 3)[:, :, None], (2, HA, 2, t, PAGE)
                              ).reshape(2, nmap * t, PAGE)
    far_a = jnp.repeat(far[:HA], 2 * t)[:, None]
    oa = _sample_attn(page_table, q_bd, cache_ka.reshape(na, n_pool, PAGE, -1),
                      cache_va.reshape(na, n_pool, PAGE, -1), pad_rows(ka), pad_rows(va), far_a, near_a, j,
                      diff=True, lam=lam, subln_g=subln_g, lam_init=lam_init)

    qi_rows = qi.reshape(bd, t, H_IDX, D_IDX).transpose(0, 2, 1, 3).reshape(bd, H_IDX * t, D_IDX)
    wi_col = wi.reshape(bd, t, H_IDX).transpose(0, 2, 1).reshape(bd, H_IDX * t, 1)
    sc = _sample_index(page_table, qi_rows, wi_col, cache_kidx, pad_rows(ki), j)
    mask = _sample_select(sc, past_len, min(TOPK_MAX, (past_len + t) // 4))

    qb_rows = qb.reshape(bd, t, HB, DH).transpose(0, 2, 1, 3).reshape(bd, HB * t, DH)
    near_b = tiles[HA:].transpose(1, 0, 2, 3).reshape(2, HB * t, PAGE)
    far_b = jnp.repeat(far[HA:], t)[:, None]
    ob = _sample_attn(page_table, qb_rows, cache_kb.reshape(na, n_pool, PAGE, -1),
                      cache_vb.reshape(na, n_pool, PAGE, -1), pad_rows(kb), pad_rows(vb), far_b, near_b, j,
                      diff=False, mask=mask)
    ob = ob.reshape(bd, HB, t, DH).transpose(0, 2, 1, 3).reshape(bd, t, HB * DH)
    o = jnp.concatenate([oa, ob], -1).reshape(bd * t, -1).astype(BF16)
    y = _matmul(o, w_out, j)
    return y, (ka, va, kb, vb, ki)


def _softplus(z):
    return jnp.maximum(z, 0.0) + jnp.log(1.0 + jnp.exp(-jnp.abs(z)))


def _rwkv(x, shift0, wkv0, j, p, *, chunk, heads, exact):
    b, t, d = x.shape
    xr, xw, xk, xv, xa, xg = [z.reshape(b * t, d) for z in _rw_prep(x, shift0, p['mix'][j])]
    r = _matmul(xr, p['wr'], j)
    k = _matmul(xk, p['wk'], j)
    v = _matmul(xv, p['wv'], j)
    w = _matmul(_matmul(xw, p['w1'], j, epilogue=jnp.tanh), p['w2'], j,
                epilogue=lambda acc, w0: -_softplus(-(w0 + acc)) - 0.5, extras=(p['w0'][j][None],))
    a = _matmul(_matmul(xa, p['a1'], j), p['a2'], j,
                epilogue=lambda acc, a0: jax.nn.sigmoid(a0 + acc), extras=(p['a0'][j][None],))
    g = _matmul(_matmul(xg, p['g1'], j, epilogue=jax.nn.sigmoid), p['g2'], j)
    sh = lambda z: z.reshape(b, t, d)
    y_pre, s_fin = _rw_scan(sh(r), sh(w), sh(k), sh(v), sh(a), sh(g), p['kk'][j], p['ka'][j], p['rk'][j],
                            p['gn_g'][j], p['gn_b'][j], wkv0, chunk=chunk, heads=heads, exact=exact)
    y = _matmul(y_pre.reshape(b * t, d), p['wo'], j)
    return y, s_fin, x[:, -1]


def kernel(x_prompt, x_sample, cache_ka, cache_va, cache_kb, cache_vb, cache_kidx, state_wkv, state_shift,
           page_table, rel_bias, w_in_att, w_out_att, lam_q1, lam_k1, lam_q2, lam_k2, subln_g,
           rw_mix, rw_w0, rw_w1, rw_w2, rw_a0, rw_a1, rw_a2, rw_g1, rw_g2, rw_kk, rw_ka, rw_rk,
           rw_wr, rw_wk, rw_wv, rw_wo, rw_gn_g, rw_gn_b, w_router, router_bias, w_gate, w_up, w_down,
           ln1_g, ln1_b, ln2_g, ln2_b):
    bp, s, d = x_prompt.shape
    bd, t, _ = x_sample.shape
    rw = dict(mix=rw_mix, w0=rw_w0, w1=rw_w1, w2=rw_w2, a0=rw_a0, a1=rw_a1, a2=rw_a2, g1=rw_g1, g2=rw_g2,
              kk=rw_kk, ka=rw_ka, rk=rw_rk, wr=rw_wr, wk=rw_wk, wv=rw_wv, wo=rw_wo, gn_g=rw_gn_g, gn_b=rw_gn_b)
    yp, ys = x_prompt, x_sample
    rows_p, rows_s = [], []
    wkv_p, shift_p, wkv_s, shift_s = [], [], [], []
    n_layers = w_gate.shape[0]
    for i in range(n_layers):
        j = i // 2
        if i % 2 == 0:
            lam_init = 0.8 - 0.6 * math.exp(-0.3 * i)
            lam = jnp.stack([lam_q1[j], lam_k1[j], lam_q2[j], lam_k2[j]], 0)
            mp, rp = _att_prompt(yp, w_in_att, w_out_att, j, rel_bias, lam, lam_init, subln_g[j][None])
            ms, rs = _att_sample(ys, j, cache_ka, cache_va, cache_kb, cache_vb, cache_kidx, page_table,
                                 w_in_att, w_out_att, rel_bias, lam, lam_init, subln_g[j][None])
            rows_p.append(rp)
            rows_s.append(rs)
        else:
            nh = d // RW_N
            mp, sp, hp = _rwkv(yp, jnp.zeros((bp, d), F32), jnp.zeros((bp, nh, RW_N, RW_N), F32), j, rw,
                               chunk=64, heads=4, exact=False)
            ms, ss, hs = _rwkv(ys, state_shift[j], state_wkv[j], j, rw, chunk=8, heads=4, exact=True)
            wkv_p.append(sp)
            shift_p.append(hp)
            wkv_s.append(ss)
            shift_s.append(hs)
        yp2 = _ln_res(yp.reshape(bp * s, d), mp, ln1_g[i], ln1_b[i])
        ys2 = _ln_res(ys.reshape(bd * t, d), ms, ln1_g[i], ln1_b[i])
        yp2 = _ln_res(yp2, _moe(yp2, w_router, router_bias, w_gate, w_up, w_down, i), ln2_g[i], ln2_b[i])
        ys2 = _ln_res(ys2, _moe(ys2, w_router, router_bias, w_gate, w_up, w_down, i), ln2_g[i], ln2_b[i])
        yp, ys = yp2.reshape(bp, s, d), ys2.reshape(bd, t, d)

    def rows(rs_list, b_, t_):
        ka, va, kb, vb, ki = (jnp.stack(z, 0) for z in zip(*rs_list))
        n = ka.shape[0]
        return (ka.reshape(n, b_, t_, HA, 2, DH), va.reshape(n, b_, t_, HA, 2 * DH),
                kb.reshape(n, b_, t_, KVB, DH), vb.reshape(n, b_, t_, KVB, DH), ki.reshape(n, b_, t_, D_IDX))

    return (yp, ys) + rows(rows_p, bp, s) + rows(rows_s, bd, t) + (
        jnp.stack(wkv_p, 0), jnp.stack(shift_p, 0), jnp.stack(wkv_s, 0), jnp.stack(shift_s, 0))
```

```python
import functools
import math

import numpy as np
import jax
import jax.numpy as jnp
from jax import lax
from jax.experimental import pallas as pl
from jax.experimental.pallas import tpu as pltpu

F32 = jnp.float32
BF16 = jnp.bfloat16
I32 = jnp.int32
HIGHEST = lax.Precision.HIGHEST

HA, DH = 8, 128
HB, KVB = 16, 2
GB = HB // KVB
H_IDX, D_IDX = 32, 128
TOPK_MAX = 256
IDX_SCALE = (H_IDX * D_IDX) ** -0.5
N_BUCKETS, BUCKET_EXACT, BUCKET_MAX_DIST = 32, 16, 128
RW_N = 64
RW_GN_EPS = 64e-5
N_EXPERTS, N_GROUPS = 16, 4
EXP_PER_GROUP = N_EXPERTS // N_GROUPS
LN_EPS = 1e-5
DEPTH = 2
ALPHA = (2 * DEPTH) ** 0.25
PAGE = 128
ATT_SCALE = DH ** -0.5
MASKED = -1e30

C_QA, C_KA, C_VA = 0, HA * 2 * DH, 2 * HA * 2 * DH
C_QB = 3 * HA * 2 * DH
C_KB = C_QB + HB * DH
C_VB = C_KB + KVB * DH
C_QI = C_VB + KVB * DH
C_KI = C_QI + H_IDX * D_IDX
C_WI = C_KI + D_IDX
IN_WIDTH = C_WI + H_IDX

LANE = 128
VMEM_LIMIT = 56 * 1024 * 1024


def _cparams(sem, vmem=VMEM_LIMIT):
    return pltpu.CompilerParams(dimension_semantics=sem, vmem_limit_bytes=vmem)


def _tile(dim, pref):
    return dim if dim <= pref else pref


def _dot(a, b):
    return jnp.dot(a.astype(BF16), b.astype(BF16), preferred_element_type=F32)


def _dot_nt(a, b):
    return lax.dot_general(a.astype(BF16), b.astype(BF16), (((1,), (1,)), ((), ())),
                           preferred_element_type=F32)


def _mm_body(*refs, nk, n_extra, epilogue, ow):
    x_ref, w_ref = refs[0], refs[1]
    extras = refs[2:2 + n_extra]
    o_ref, acc_ref = refs[2 + n_extra], refs[3 + n_extra]
    k = pl.program_id(2)

    @pl.when(k == 0)
    def _init():
        acc_ref[...] = jnp.zeros_like(acc_ref)

    acc_ref[...] += _dot(x_ref[...], w_ref[...])

    @pl.when(k == nk - 1)
    def _finish():
        acc = acc_ref[...][:, :ow]
        if epilogue is not None:
            acc = epilogue(acc, *[e[...] for e in extras])
        o_ref[...] = acc.astype(o_ref.dtype)


def _matmul(x, w, layer, *, col0=0, ncols=None, out_dtype=F32, epilogue=None, extras=(),
            tm=1024, tn=1024, tk=512):
    m, kdim = x.shape
    nfull = w.shape[-1]
    ncols = nfull if ncols is None else ncols
    tm = _tile(m, tm)
    tk = _tile(kdim, tk)
    assert m % tm == 0 and kdim % tk == 0
    if ncols <= tn:
        if col0 == 0 and ncols == nfull:
            tnw = ncols
        else:
            tnw = -(-ncols // LANE) * LANE
        tno = ncols
    else:
        tnw = tno = tn
    assert col0 % tnw == 0
    cb = col0 // tnw
    nk = kdim // tk
    grid = (m // tm, pl.cdiv(ncols, tno), nk)
    in_specs = [pl.BlockSpec((tm, tk), lambda i, j, k: (i, k)),
                pl.BlockSpec((None, tk, tnw), lambda i, j, k: (layer, k, j + cb))]
    for _ in extras:
        in_specs.append(pl.BlockSpec((1, tno), lambda i, j, k: (0, j)))
    return pl.pallas_call(
        functools.partial(_mm_body, nk=nk, n_extra=len(extras), epilogue=epilogue, ow=tno),
        grid=grid,
        in_specs=in_specs,
        out_specs=pl.BlockSpec((tm, tno), lambda i, j, k: (i, j)),
        out_shape=jax.ShapeDtypeStruct((m, ncols), out_dtype),
        scratch_shapes=[pltpu.VMEM((tm, tnw), F32)],
        compiler_params=_cparams(("parallel", "parallel", "arbitrary")),
        name="matmul",
    )(x, w, *extras)


def _ln_rows(z, g, b):
    mu = jnp.mean(z, -1, keepdims=True)
    zc = z - mu
    var = jnp.mean(zc * zc, -1, keepdims=True)
    return zc * lax.rsqrt(var + LN_EPS) * g + b


def _ln_body(x_ref, m_ref, g_ref, b_ref, o_ref):
    o_ref[...] = _ln_rows(ALPHA * x_ref[...] + m_ref[...], g_ref[...], b_ref[...])


def _ln_res(x, m, g, b):
    n, d = x.shape
    tm = _tile(n, 256)
    row = pl.BlockSpec((tm, d), lambda i: (i, 0))
    vec = pl.BlockSpec((1, d), lambda i: (0, 0))
    return pl.pallas_call(
        _ln_body, grid=(n // tm,), in_specs=[row, row, vec, vec], out_specs=row,
        out_shape=jax.ShapeDtypeStruct((n, d), F32),
        compiler_params=_cparams(("parallel",)),
        name="ln_residual",
    )(x, m, g[None], b[None])


def _t5_bucket_np(dist):
    n = np.maximum(dist, 0)
    nf = np.maximum(n, 1).astype(np.float32)
    large = BUCKET_EXACT + (np.log(nf / np.float32(BUCKET_EXACT)) / np.float32(math.log(BUCKET_MAX_DIST / BUCKET_EXACT))
                            * np.float32(N_BUCKETS - BUCKET_EXACT)).astype(np.int32)
    large = np.minimum(large, N_BUCKETS - 1)
    return np.where(n < BUCKET_EXACT, n, large).astype(np.int32)


def _bias_body(rel_ref, bkt_ref, o_ref):
    h = pl.program_id(0)
    bkt = bkt_ref[...]
    out = jnp.zeros(bkt.shape, F32)
    for b in range(N_BUCKETS):
        out = jnp.where(bkt == b, rel_ref[b, h], out)
    o_ref[0] = out


def _bias_tiles(rel_bias, buckets):
    nh = rel_bias.shape[1]
    n, r, c = buckets.shape
    return pl.pallas_call(
        _bias_body, grid=(nh,),
        in_specs=[pl.BlockSpec(memory_space=pltpu.SMEM),
                  pl.BlockSpec((n, r, c), lambda h: (0, 0, 0))],
        out_specs=pl.BlockSpec((1, n, r, c), lambda h: (h, 0, 0, 0)),
        out_shape=jax.ShapeDtypeStruct((nh, n, r, c), F32),
        compiler_params=_cparams(("arbitrary",)),
        name="bias_tiles",
    )(rel_bias, jnp.asarray(buckets))


def _softmax_step(s, valid, m_old, l_old):
    if valid is not None:
        s = jnp.where(valid, s, MASKED)
    m_new = jnp.maximum(m_old, jnp.max(s, -1, keepdims=True))
    p = jnp.exp(s - m_new)
    if valid is not None:
        p = jnp.where(valid, p, 0.0)
    alpha = jnp.exp(m_old - m_new)
    l_new = alpha * l_old + jnp.sum(p, -1, keepdims=True)
    return p, alpha, m_new, l_new


def _diff_finish(o1, o2, lv, g, lam_init):
    lam = (jnp.exp(jnp.sum(lv[0:1] * lv[1:2], -1, keepdims=True))
           - jnp.exp(jnp.sum(lv[2:3] * lv[3:4], -1, keepdims=True)) + lam_init)
    o = o1 - lam * o2
    o = o * lax.rsqrt(jnp.mean(o * o, -1, keepdims=True) + 1e-5)
    return o * g * (1.0 - lam_init)


def _tri_tables(nq):
    iq = np.concatenate([np.full(i + 1, i) for i in range(nq)])
    ik = np.concatenate([np.arange(i + 1) for i in range(nq)])
    return jnp.asarray(iq, I32), jnp.asarray(ik, I32)


def _pdiff_body(iq_ref, ik_ref, rel_ref, lam_ref, q_ref, k_ref, v_ref, bias_ref, g_ref, o_ref,
                m_ref, l_ref, acc_ref, *, t, lam_init):
    h, st = pl.program_id(1), pl.program_id(2)
    iq, ik = iq_ref[st], ik_ref[st]

    @pl.when(ik == 0)
    def _init():
        m_ref[...] = jnp.full(m_ref.shape, MASKED, F32)
        l_ref[...] = jnp.zeros_like(l_ref)
        acc_ref[...] = jnp.zeros_like(acc_ref)

    q = q_ref[0]
    k = k_ref[0]
    v = v_ref[0].astype(BF16)
    bias = jnp.where(ik >= iq - 1, bias_ref[0, 0], rel_ref[N_BUCKETS - 1, h])
    row = lax.broadcasted_iota(I32, (t, t), 0)
    col = lax.broadcasted_iota(I32, (t, t), 1)
    valid = jnp.logical_or(ik < iq, col <= row)
    for c in range(2):
        s = _dot_nt(q[:, c * DH:(c + 1) * DH], k[:, c * DH:(c + 1) * DH]) * ATT_SCALE + bias
        p, alpha, m_new, l_new = _softmax_step(s, valid, m_ref[c], l_ref[c])
        acc_ref[c] = alpha * acc_ref[c] + jnp.dot(p.astype(BF16), v, preferred_element_type=F32)
        m_ref[c] = m_new
        l_ref[c] = l_new

    @pl.when(ik == iq)
    def _finish():
        o = _diff_finish(acc_ref[0] / l_ref[0], acc_ref[1] / l_ref[1], lam_ref[...], g_ref[...], lam_init)
        o_ref[0] = o.astype(o_ref.dtype)


def _prompt_diff(qa, ka, va, bias_a, rel_bias, lam, subln_g, lam_init, t):
    b, s, _ = qa.shape
    iq_tab, ik_tab = _tri_tables(s // t)
    wv = 2 * DH
    grid_spec = pltpu.PrefetchScalarGridSpec(
        num_scalar_prefetch=2, grid=(b, HA, iq_tab.shape[0]),
        in_specs=[pl.BlockSpec(memory_space=pltpu.SMEM),
                  pl.BlockSpec((4, DH), lambda b_, h, st, iq, ik: (0, 0)),
                  pl.BlockSpec((1, t, wv), lambda b_, h, st, iq, ik: (b_, iq[st], h)),
                  pl.BlockSpec((1, t, wv), lambda b_, h, st, iq, ik: (b_, ik[st], h)),
                  pl.BlockSpec((1, t, wv), lambda b_, h, st, iq, ik: (b_, ik[st], h)),
                  pl.BlockSpec((1, 1, t, t), lambda b_, h, st, iq, ik: (h, jnp.where(ik[st] == iq[st], 0, 1), 0, 0)),
                  pl.BlockSpec((1, wv), lambda b_, h, st, iq, ik: (0, 0))],
        out_specs=pl.BlockSpec((1, t, wv), lambda b_, h, st, iq, ik: (b_, iq[st], h)),
        scratch_shapes=[pltpu.VMEM((2, t, 1), F32), pltpu.VMEM((2, t, 1), F32), pltpu.VMEM((2, t, wv), F32)])
    return pl.pallas_call(
        functools.partial(_pdiff_body, t=t, lam_init=lam_init),
        grid_spec=grid_spec,
        out_shape=jax.ShapeDtypeStruct((b, s, HA * wv), BF16),
        compiler_params=_cparams(("parallel", "parallel", "arbitrary")),
        name="prompt_diff_attention",
    )(iq_tab, ik_tab, rel_bias, lam, qa, ka, va, bias_a, subln_g)


def _count(cond):
    return jnp.sum(jnp.where(cond, 1.0, 0.0), -1, keepdims=True)


def _topk_mask(sc, k, n_idx_bits):
    bits = lax.bitcast_convert_type(sc, I32)
    key = bits ^ ((bits >> 31) & jnp.int32(0x7FFFFFFF))
    kf = jnp.float32(k)
    int_min = jnp.int32(-2 ** 31)
    thr = jnp.where(_count(key >= 0) >= kf, jnp.int32(0), int_min)
    for bit in range(30, -1, -1):
        cand = thr | jnp.int32(1 << bit)
        thr = jnp.where(_count(key >= cand) >= kf, cand, thr)
    above = key > thr
    tie = key == thr
    need = kf - _count(above)
    idx = lax.broadcasted_iota(I32, sc.shape, sc.ndim - 1)
    last = jnp.zeros_like(thr)
    for bit in range(n_idx_bits - 1, -1, -1):
        cand = last | jnp.int32(1 << bit)
        last = jnp.where(_count(jnp.logical_and(tie, idx < cand)) < need, cand, last)
    return jnp.logical_or(above, jnp.logical_and(tie, idx <= last))


def _pindex_body(qi_ref, ki_ref, wi_ref, o_ref, *, tq, s, topk, nbits):
    iq = pl.program_id(1)
    ki = ki_ref[0].astype(BF16)
    wi = wi_ref[0]
    sc = jnp.zeros((tq, s), F32)
    for h in range(H_IDX):
        d = lax.dot_general(qi_ref[0, :, h * D_IDX:(h + 1) * D_IDX], ki, (((1,), (1,)), ((), ())),
                            preferred_element_type=F32)
        sc = sc + jnp.maximum(d, 0.0) * wi[:, h:h + 1]
    sc = sc * IDX_SCALE
    qpos = iq * tq + lax.broadcasted_iota(I32, (tq, s), 0)
    kpos = lax.broadcasted_iota(I32, (tq, s), 1)
    causal = kpos <= qpos
    sc = jnp.where(causal, sc, -jnp.inf)
    sel = jnp.logical_and(_topk_mask(sc, topk, nbits), causal)
    o_ref[0] = jnp.where(sel, 1.0, 0.0)


def _prompt_select(qi, ki, wi, topk):
    b, s, _ = qi.shape
    tq = _tile(s, 128)
    nbits = int(s).bit_length()
    return pl.pallas_call(
        functools.partial(_pindex_body, tq=tq, s=s, topk=topk, nbits=nbits),
        grid=(b, s // tq),
        in_specs=[pl.BlockSpec((1, tq, H_IDX * D_IDX), lambda b_, i: (b_, i, 0)),
                  pl.BlockSpec((1, s, D_IDX), lambda b_, i: (b_, 0, 0)),
                  pl.BlockSpec((1, tq, H_IDX), lambda b_, i: (b_, i, 0))],
        out_specs=pl.BlockSpec((1, tq, s), lambda b_, i: (b_, i, 0)),
        out_shape=jax.ShapeDtypeStruct((b, s, s), F32),
        compiler_params=_cparams(("parallel", "parallel")),
        name="prompt_indexer_topk_mask",
    )(qi, ki, wi)


def _psparse_body(iq_ref, ik_ref, rel_ref, q_ref, k_ref, v_ref, msk_ref, bias_ref, o_ref,
                  m_ref, l_ref, acc_ref):
    g, st = pl.program_id(1), pl.program_id(2)
    iq, ik = iq_ref[st], ik_ref[st]

    @pl.when(ik == 0)
    def _init():
        m_ref[...] = jnp.full(m_ref.shape, MASKED, F32)
        l_ref[...] = jnp.zeros_like(l_ref)
        acc_ref[...] = jnp.zeros_like(acc_ref)

    k = k_ref[0].astype(BF16)
    v = v_ref[0].astype(BF16)
    valid = msk_ref[0] > 0.5
    near = ik >= iq - 1
    for r in range(GB):
        bias = jnp.where(near, bias_ref[r, 0], rel_ref[N_BUCKETS - 1, HA + g * GB + r])
        s = _dot_nt(q_ref[0, :, r * DH:(r + 1) * DH], k) * ATT_SCALE + bias
        p, alpha, m_new, l_new = _softmax_step(s, valid, m_ref[r], l_ref[r])
        acc_ref[r] = alpha * acc_ref[r] + jnp.dot(p.astype(BF16), v, preferred_element_type=F32)
        m_ref[r] = m_new
        l_ref[r] = l_new

    @pl.when(ik == iq)
    def _finish():
        for r in range(GB):
            o_ref[0, :, r * DH:(r + 1) * DH] = (acc_ref[r] / l_ref[r]).astype(o_ref.dtype)


def _prompt_sparse(qb, kb, vb, mask, bias_b, rel_bias, t):
    b, s, _ = qb.shape
    iq_tab, ik_tab = _tri_tables(s // t)
    grid_spec = pltpu.PrefetchScalarGridSpec(
        num_scalar_prefetch=2, grid=(b, KVB, iq_tab.shape[0]),
        in_specs=[pl.BlockSpec(memory_space=pltpu.SMEM),
                  pl.BlockSpec((1, t, GB * DH), lambda b_, g, st, iq, ik: (b_, iq[st], g)),
                  pl.BlockSpec((1, t, DH), lambda b_, g, st, iq, ik: (b_, ik[st], g)),
                  pl.BlockSpec((1, t, DH), lambda b_, g, st, iq, ik: (b_, ik[st], g)),
                  pl.BlockSpec((1, t, t), lambda b_, g, st, iq, ik: (b_, iq[st], ik[st])),
                  pl.BlockSpec((GB, 1, t, t),
                               lambda b_, g, st, iq, ik: (g, jnp.where(ik[st] == iq[st], 0, 1), 0, 0))],
        out_specs=pl.BlockSpec((1, t, GB * DH), lambda b_, g, st, iq, ik: (b_, iq[st], g)),
        scratch_shapes=[pltpu.VMEM((GB, t, 1), F32), pltpu.VMEM((GB, t, 1), F32), pltpu.VMEM((GB, t, DH), F32)])
    return pl.pallas_call(
        _psparse_body, grid_spec=grid_spec,
        out_shape=jax.ShapeDtypeStruct((b, s, HB * DH), BF16),
        compiler_params=_cparams(("parallel", "parallel", "arbitrary")),
        name="prompt_sparse_attention",
    )(iq_tab, ik_tab, rel_bias, qb, kb, vb, mask, bias_b)


def _page_cat(ref, nsub):
    parts = [ref[0, 0, pl.ds(i, PAGE, stride=nsub), :].astype(BF16) for i in range(nsub)]
    return parts[0] if nsub == 1 else jnp.concatenate(parts, 1)


def _index_scores(q, w, keys):
    r = jnp.maximum(_dot_nt(q, keys), 0.0) * w
    return jnp.sum(r.reshape(H_IDX, 8, keys.shape[0]), 0) * IDX_SCALE


def _sindex_body(pt_ref, q_ref, w_ref, *refs, gp):
    kc_refs, o_ref = refs[:gp], refs[gp]
    keys = jnp.concatenate([_page_cat(r, 1) for r in kc_refs], 0)
    o_ref[0] = _index_scores(q_ref[0], w_ref[0], keys)


def _sample_index(page_table, qi_rows, wi_col, cache_kidx, layer, gp):
    bd, n_pages = page_table.shape
    rows = qi_rows.shape[1]

    def page(i):
        return lambda b, p, pt: (layer, pt[b, p * gp + i], 0, 0)

    grid_spec = pltpu.PrefetchScalarGridSpec(
        num_scalar_prefetch=1, grid=(bd, n_pages // gp),
        in_specs=[pl.BlockSpec((1, rows, D_IDX), lambda b, p, pt: (b, 0, 0)),
                  pl.BlockSpec((1, rows, 1), lambda b, p, pt: (b, 0, 0))]
        + [pl.BlockSpec((1, 1, PAGE, D_IDX), page(i)) for i in range(gp)],
        out_specs=pl.BlockSpec((1, 8, gp * PAGE), lambda b, p, pt: (b, 0, p)))
    return pl.pallas_call(
        functools.partial(_sindex_body, gp=gp), grid_spec=grid_spec,
        out_shape=jax.ShapeDtypeStruct((bd, 8, n_pages * PAGE), F32),
        compiler_params=_cparams(("parallel", "arbitrary")),
        name="sample_indexer_scores",
    )(page_table, qi_rows, wi_col, *([cache_kidx] * gp))


def _sselect_body(sc_ref, q_ref, w_ref, kn_ref, oc_ref, on_ref, *, past_len, topk, nbits):
    sc_new = _index_scores(q_ref[0], w_ref[0], kn_ref[0].astype(BF16))
    sc = jnp.concatenate([sc_ref[0], sc_new], 1)
    qpos = past_len + lax.broadcasted_iota(I32, sc.shape, 0)
    kpos = lax.broadcasted_iota(I32, sc.shape, 1)
    causal = kpos <= qpos
    sc = jnp.where(causal, sc, -jnp.inf)
    sel = jnp.where(jnp.logical_and(_topk_mask(sc, topk, nbits), causal), 1.0, 0.0)
    oc_ref[0] = sel[:, :past_len]
    on_ref[0] = sel[:, past_len:]


def _sample_select(sc, qi_rows, wi_col, ki_new, topk):
    bd, t, past_len = sc.shape
    rows = qi_rows.shape[1]
    per_b = lambda b: (b, 0, 0)
    return pl.pallas_call(
        functools.partial(_sselect_body, past_len=past_len, topk=topk,
                          nbits=int(past_len + PAGE).bit_length()),
        grid=(bd,),
        in_specs=[pl.BlockSpec((1, t, past_len), per_b), pl.BlockSpec((1, rows, D_IDX), per_b),
                  pl.BlockSpec((1, rows, 1), per_b), pl.BlockSpec((1, PAGE, D_IDX), per_b)],
        out_specs=[pl.BlockSpec((1, t, past_len), per_b), pl.BlockSpec((1, t, PAGE), per_b)],
        out_shape=[jax.ShapeDtypeStruct((bd, t, past_len), F32), jax.ShapeDtypeStruct((bd, t, PAGE), F32)],
        compiler_params=_cparams(("parallel",)),
        name="sample_topk_mask",
    )(sc, qi_rows, wi_col, ki_new)


def _ssparse_body(pt_ref, q_ref, *refs, n_steps, gp):
    kc_refs, vc_refs = refs[:gp], refs[gp:2 * gp]
    kn_ref, vn_ref, far_ref, near_ref, mc_ref, mn_ref, o_ref, m_ref, l_ref, acc_ref = refs[2 * gp:]
    p = pl.program_id(1)

    @pl.when(p == 0)
    def _init():
        m_ref[...] = jnp.full(m_ref.shape, MASKED, F32)
        l_ref[...] = jnp.zeros_like(l_ref)
        acc_ref[...] = jnp.zeros_like(acc_ref)

    q = q_ref[0]
    rows = q.shape[0]
    half = rows // KVB

    def process(k, v, bias, valid):
        s = jnp.concatenate([_dot_nt(q[g * half:(g + 1) * half], k[:, g * DH:(g + 1) * DH])
                             for g in range(KVB)], 0)
        s = s * ATT_SCALE + bias
        pr, alpha, m_new, l_new = _softmax_step(s, valid, m_ref[...], l_ref[...])
        pr = pr.astype(BF16)
        pv = jnp.concatenate([jnp.dot(pr[g * half:(g + 1) * half], v[:, g * DH:(g + 1) * DH],
                                      preferred_element_type=F32) for g in range(KVB)], 0)
        acc_ref[...] = alpha * acc_ref[...] + pv
        m_ref[...] = m_new
        l_ref[...] = l_new

    def row_mask(m):
        return jnp.concatenate([m] * (rows // 8), 0) > 0.5

    @pl.when(p < n_steps)
    def _cache():
        k = jnp.concatenate([_page_cat(r, KVB) for r in kc_refs], 0)
        v = jnp.concatenate([_page_cat(r, KVB) for r in vc_refs], 0)
        far = jnp.broadcast_to(far_ref[...], (rows, PAGE))
        last = jnp.where(p == n_steps - 1, near_ref[0], far)
        bias = jnp.concatenate([far] * (gp - 1) + [last], 1)
        process(k, v, bias, row_mask(mc_ref[0]))

    @pl.when(p == n_steps)
    def _new():
        tok = lax.broadcasted_iota(I32, (rows, PAGE), 0) % 8
        col = lax.broadcasted_iota(I32, (rows, PAGE), 1)
        valid = jnp.logical_and(col <= tok, row_mask(mn_ref[0]))
        process(kn_ref[0].astype(BF16), vn_ref[0].astype(BF16), near_ref[1], valid)
        o_ref[0] = acc_ref[...] / l_ref[...]


def _sample_sparse(page_table, q_rows, kcache, vcache, knew, vnew, far_col, near, mask_c, mask_n, layer, gp):
    bd, n_pages = page_table.shape
    n_steps = n_pages // gp
    rows = q_rows.shape[1]

    def page(i):
        return lambda b, p, pt: (layer, pt[b, jnp.minimum(p, n_steps - 1) * gp + i], 0, 0)

    per_b = lambda b, p, pt: (b, 0, 0)
    cache_spec = [pl.BlockSpec((1, 1, PAGE * KVB, DH), page(i)) for i in range(gp)]
    in_specs = ([pl.BlockSpec((1, rows, DH), per_b)] + cache_spec + cache_spec
                + [pl.BlockSpec((1, PAGE, KVB * DH), per_b),
                   pl.BlockSpec((1, PAGE, KVB * DH), per_b),
                   pl.BlockSpec((rows, 1), lambda b, p, pt: (0, 0)),
                   pl.BlockSpec((2, rows, PAGE), lambda b, p, pt: (0, 0, 0)),
                   pl.BlockSpec((1, 8, gp * PAGE), lambda b, p, pt: (b, 0, jnp.minimum(p, n_steps - 1))),
                   pl.BlockSpec((1, 8, PAGE), per_b)])
    grid_spec = pltpu.PrefetchScalarGridSpec(
        num_scalar_prefetch=1, grid=(bd, n_steps + 1), in_specs=in_specs,
        out_specs=pl.BlockSpec((1, rows, DH), per_b),
        scratch_shapes=[pltpu.VMEM((rows, 1), F32), pltpu.VMEM((rows, 1), F32), pltpu.VMEM((rows, DH), F32)])
    return pl.pallas_call(
        functools.partial(_ssparse_body, n_steps=n_steps, gp=gp),
        grid_spec=grid_spec, out_shape=jax.ShapeDtypeStruct((bd, rows, DH), F32),
        compiler_params=_cparams(("parallel", "arbitrary")),
        name="sample_sparse_attention",
    )(page_table, q_rows, *([kcache] * gp), *([vcache] * gp), knew, vnew, far_col, near, mask_c, mask_n)


def _sdiff_body(pt_ref, lam_ref, q_ref, *refs, n_steps, gp, lam_init):
    kc_refs, vc_refs = refs[:gp], refs[gp:2 * gp]
    kn_ref, vn_ref, far_ref, near_ref, nearn_ref, g_ref, o_ref, m_ref, l_ref, acc_ref = refs[2 * gp:]
    p = pl.program_id(1)
    rows = HA * 8

    @pl.when(p == 0)
    def _init():
        m_ref[...] = jnp.full(m_ref.shape, MASKED, F32)
        l_ref[...] = jnp.zeros_like(l_ref)
        acc_ref[...] = jnp.zeros_like(acc_ref)

    def process(c, k, v, bias, valid):
        s = _dot_nt(q_ref[0, c], k) * ATT_SCALE + bias
        pr, alpha, m_new, l_new = _softmax_step(s, valid, m_ref[c], l_ref[c])
        acc_ref[c] = alpha * acc_ref[c] + jnp.dot(pr.astype(BF16), v, preferred_element_type=F32)
        m_ref[c] = m_new
        l_ref[c] = l_new

    def own_head(ncol):
        row = lax.broadcasted_iota(I32, (rows, ncol), 0)
        col = lax.broadcasted_iota(I32, (rows, ncol), 1)
        return (col % HA) == (row // 8), row, col

    @pl.when(p < n_steps)
    def _cache():
        valid, _, _ = own_head(gp * PAGE * HA)
        v = jnp.concatenate([r[0, 0].astype(BF16) for r in vc_refs], 0)
        far = jnp.broadcast_to(far_ref[...], (rows, PAGE * HA))
        last = jnp.where(p == n_steps - 1, near_ref[...], far)
        bias = jnp.concatenate([far] * (gp - 1) + [last], 1)
        for c in range(2):
            k = jnp.concatenate([r[0, 0, pl.ds(c, PAGE * HA, stride=2), :].astype(BF16) for r in kc_refs], 0)
            process(c, k, v, bias, valid)

    @pl.when(p == n_steps)
    def _new():
        valid, row, col = own_head(8 * HA)
        valid = jnp.logical_and(valid, col // HA <= row % 8)
        v = vn_ref[0].astype(BF16)
        for c in range(2):
            process(c, kn_ref[0, c].astype(BF16), v, nearn_ref[...], valid)
        wv = 2 * DH
        o1 = acc_ref[0] / l_ref[0]
        o2 = acc_ref[1] / l_ref[1]
        for h in range(HA):
            o_ref[0, :, h * wv:(h + 1) * wv] = _diff_finish(o1[h * 8:(h + 1) * 8], o2[h * 8:(h + 1) * 8],
                                                            lam_ref[...], g_ref[...], lam_init)


def _sample_diff(page_table, q_rows, kcache, vcache, knew, vnew, far_col, near_last, near_new, lam, subln_g,
                 lam_init, layer, gp):
    bd, n_pages = page_table.shape
    n_steps = n_pages // gp
    rows = HA * 8
    wv = 2 * DH

    def page(i):
        return lambda b, p, pt: (layer, pt[b, jnp.minimum(p, n_steps - 1) * gp + i], 0, 0)

    per_b3 = lambda b, p, pt: (b, 0, 0)
    per_b4 = lambda b, p, pt: (b, 0, 0, 0)
    const2 = lambda b, p, pt: (0, 0)
    in_specs = ([pl.BlockSpec((4, DH), const2), pl.BlockSpec((1, 2, rows, DH), per_b4)]
                + [pl.BlockSpec((1, 1, PAGE * HA * 2, DH), page(i)) for i in range(gp)]
                + [pl.BlockSpec((1, 1, PAGE * HA, wv), page(i)) for i in range(gp)]
                + [pl.BlockSpec((1, 2, rows, DH), per_b4),
                   pl.BlockSpec((1, rows, wv), per_b3),
                   pl.BlockSpec((rows, 1), const2),
                   pl.BlockSpec((rows, PAGE * HA), const2),
                   pl.BlockSpec((rows, 8 * HA), const2),
                   pl.BlockSpec((1, wv), const2)])
    grid_spec = pltpu.PrefetchScalarGridSpec(
        num_scalar_prefetch=1, grid=(bd, n_steps + 1), in_specs=in_specs,
        out_specs=pl.BlockSpec((1, 8, HA * wv), per_b3),
        scratch_shapes=[pltpu.VMEM((2, rows, 1), F32), pltpu.VMEM((2, rows, 1), F32), pltpu.VMEM((2, rows, wv), F32)])
    return pl.pallas_call(
        functools.partial(_sdiff_body, n_steps=n_steps, gp=gp, lam_init=lam_init),
        grid_spec=grid_spec, out_shape=jax.ShapeDtypeStruct((bd, 8, HA * wv), F32),
        compiler_params=_cparams(("parallel", "arbitrary")),
        name="sample_diff_attention",
    )(page_table, lam, q_rows, *([kcache] * gp), *([vcache] * gp), knew, vnew, far_col, near_last, near_new,
      subln_g)


def _rwprep_body(x_ref, p8_ref, sh_ref, mix_ref, *o_refs):
    it = pl.program_id(1)
    x = x_ref[0]
    prev_last = jnp.where(it == 0, sh_ref[0], p8_ref[0, 7:8, :])
    row = lax.broadcasted_iota(I32, x.shape, 0)
    x_prev = jnp.where(row == 0, prev_last, pltpu.roll(x, 1, 0))
    xx = x_prev - x
    for c, o_ref in enumerate(o_refs):
        o_ref[0] = (x + xx * mix_ref[c:c + 1, :]).astype(o_ref.dtype)


def _rw_prep(x, shift0, mix):
    b, t, d = x.shape
    tt = _tile(t, 512)
    td = _tile(d, 1024)
    blk = pl.BlockSpec((1, tt, td), lambda b_, i, j: (b_, i, j))
    return pl.pallas_call(
        _rwprep_body, grid=(b, t // tt, d // td),
        in_specs=[blk,
                  pl.BlockSpec((1, 8, td), lambda b_, i, j: (b_, jnp.maximum(i * (tt // 8) - 1, 0), j)),
                  pl.BlockSpec((1, 1, td), lambda b_, i, j: (b_, 0, j)),
                  pl.BlockSpec((6, td), lambda b_, i, j: (0, j))],
        out_specs=[blk] * 6,
        out_shape=[jax.ShapeDtypeStruct((b, t, d), BF16)] * 6,
        compiler_params=_cparams(("parallel", "parallel", "parallel")),
        name="rwkv_token_shift_mix",
    )(x, x, shift0[:, None], mix)


def _scan_body(r_ref, w_ref, k_ref, v_ref, a_ref, g_ref, kk_ref, ka_ref, rk_ref, gg_ref, gb_ref, s0_ref,
               y_ref, sf_ref, st_ref, *, heads, c, nchunks, exact):
    ic = pl.program_id(2)

    @pl.when(ic == 0)
    def _init():
        st_ref[...] = s0_ref[0]

    if exact:
        def mm(a, b, dims=(((1,), (0,)), ((), ()))):
            return lax.dot_general(a, b, dims, precision=HIGHEST, preferred_element_type=F32)
    else:
        def mm(a, b, dims=(((1,), (0,)), ((), ()))):
            return lax.dot_general(a.astype(BF16), b.astype(BF16), dims, preferred_element_type=F32)
    nt = (((1,), (1,)), ((), ()))
    tn = (((0,), (0,)), ((), ()))
    hs = range(heads)
    sl = [slice(h * RW_N, (h + 1) * RW_N) for h in hs]

    row = lax.broadcasted_iota(I32, (c, 2 * c), 0)
    col = lax.broadcasted_iota(I32, (c, 2 * c), 1) % c
    strict = row > col
    incl = row >= col
    r_all, k_all, v_all, a_all = r_ref[0], k_ref[0], v_ref[0], a_ref[0]
    lw = -jnp.exp(w_ref[0])
    cum = jnp.dot(jnp.where(incl[:, :c], 1.0, 0.0), lw, precision=HIGHEST, preferred_element_type=F32)
    cum_end = cum[c - 1:c, :]
    kkraw = k_all * kk_ref[...]
    kmod = k_all * (1.0 + (a_all - 1.0) * ka_ref[...])
    e_neg = jnp.exp(-cum)
    e_end = jnp.exp(cum_end - cum)
    e_ex = jnp.exp(cum - lw)
    d_end = jnp.exp(cum_end)
    r_in = r_all * jnp.exp(cum)
    k_neg = kmod * e_neg
    k_end = kmod * e_end

    kk = []
    for h in hs:
        x = kkraw[:, sl[h]]
        kk.append(x * lax.rsqrt(jnp.maximum(jnp.sum(x * x, -1, keepdims=True), 1e-24)))
    bb = [kk[h] * a_all[:, sl[h]] for h in hs]
    v = [v_all[:, sl[h]] for h in hs]
    left = [jnp.concatenate([kk[h] * e_ex[:, sl[h]], r_in[:, sl[h]]], 0) for h in hs]
    right = [jnp.concatenate([bb[h] * e_neg[:, sl[h]], k_neg[:, sl[h]]], 0) for h in hs]
    s0 = [st_ref[h] for h in hs]
    amat = [mm(left[h], right[h], nt) for h in hs]
    ps = [mm(left[h], s0[h], nt) for h in hs]
    top = [jnp.where(strict, amat[h][:c], 0.0) for h in hs]
    bot = [jnp.where(incl, amat[h][c:], 0.0) for h in hs]
    x = [-(ps[h][:c] + mm(top[h][:, c:], v[h])) for h in hs]
    mp = [top[h][:, :c] for h in hs]
    x = [x[h] - mm(mp[h], x[h]) for h in hs]
    pw = 1
    while 2 * pw < c:
        mp = [mm(mp[h], mp[h]) for h in hs]
        pw *= 2
        x = [x[h] + mm(mp[h], x[h]) for h in hs]
    xv = [jnp.concatenate([x[h], v[h]], 0) for h in hs]
    o = [ps[h][c:] + mm(bot[h], xv[h]) for h in hs]
    upd = [jnp.concatenate([bb[h] * e_end[:, sl[h]], k_end[:, sl[h]]], 0) for h in hs]
    for h in hs:
        st_ref[h] = s0[h] * d_end[:, sl[h]] + mm(xv[h], upd[h], tn)
    ys = []
    for h in hs:
        mu = jnp.mean(o[h], -1, keepdims=True)
        oc = o[h] - mu
        var = jnp.mean(oc * oc, -1, keepdims=True)
        on = oc * lax.rsqrt(var + RW_GN_EPS) * gg_ref[:, sl[h]] + gb_ref[:, sl[h]]
        bonus = jnp.sum(r_all[:, sl[h]] * kmod[:, sl[h]] * rk_ref[:, sl[h]], -1, keepdims=True) * v[h]
        ys.append(on + bonus)
    y_ref[0] = (jnp.concatenate(ys, -1) * g_ref[0]).astype(y_ref.dtype)

    @pl.when(ic == nchunks - 1)
    def _finish():
        sf_ref[0] = st_ref[...]


def _rw_scan(r, w, k, v, a, g, k_k, k_a, r_k, gn_g, gn_b, state0, *, chunk, heads, exact):
    b, t, d = r.shape
    nh = d // RW_N
    c = _tile(t, chunk)
    heads = min(heads, nh)
    lanes = heads * RW_N
    seq = pl.BlockSpec((1, c, lanes), lambda b_, hg, ic: (b_, ic, hg))
    vec = pl.BlockSpec((1, lanes), lambda b_, hg, ic: (0, hg))
    st = pl.BlockSpec((1, heads, RW_N, RW_N), lambda b_, hg, ic: (b_, hg, 0, 0))
    return pl.pallas_call(
        functools.partial(_scan_body, heads=heads, c=c, nchunks=t // c, exact=exact),
        grid=(b, nh // heads, t // c),
        in_specs=[seq] * 6 + [vec] * 5 + [st],
        out_specs=[seq, st],
        out_shape=[jax.ShapeDtypeStruct((b, t, d), BF16), jax.ShapeDtypeStruct((b, nh, RW_N, RW_N), F32)],
        scratch_shapes=[pltpu.VMEM((heads, RW_N, RW_N), F32)],
        compiler_params=_cparams(("parallel", "parallel", "arbitrary")),
        name="rwkv_chunked_scan",
    )(r, w, k, v, a, g, k_k[None], k_a[None], r_k.reshape(1, d), gn_g[None], gn_b[None], state0)


def _route(x, w, b):
    logits = jnp.dot(x, w, precision=HIGHEST, preferred_element_type=F32)
    scores = jax.nn.sigmoid(logits)
    sel = scores + b
    sc = [sel[:, e:e + 1] for e in range(N_EXPERTS)]
    raw = [scores[:, e:e + 1] for e in range(N_EXPERTS)]
    best = jnp.zeros_like(sc[0]).astype(I32)
    best_score = None
    for gi in range(N_GROUPS):
        m = sc[gi * EXP_PER_GROUP:(gi + 1) * EXP_PER_GROUP]
        pair = None
        for i in range(EXP_PER_GROUP):
            for j in range(i + 1, EXP_PER_GROUP):
                pair = m[i] + m[j] if pair is None else jnp.maximum(pair, m[i] + m[j])
        if best_score is None:
            best_score = pair
        else:
            upd = pair > best_score
            best = jnp.where(upd, gi, best)
            best_score = jnp.where(upd, pair, best_score)

    def pick(vals, i):
        out = vals[i]
        for gi in range(1, N_GROUPS):
            out = jnp.where(best == gi, vals[gi * EXP_PER_GROUP + i], out)
        return out

    a = [pick(sc, i) for i in range(EXP_PER_GROUP)]
    cw = [pick(raw, i) for i in range(EXP_PER_GROUP)]
    i1, v1, w1 = jnp.zeros_like(best), a[0], cw[0]
    for i in range(1, EXP_PER_GROUP):
        upd = a[i] > v1
        i1, v1, w1 = jnp.where(upd, i, i1), jnp.where(upd, a[i], v1), jnp.where(upd, cw[i], w1)
    i2, v2, w2 = None, None, None
    for i in range(EXP_PER_GROUP):
        ok = i1 != i
        if v2 is None:
            i2, v2, w2 = jnp.full_like(best, i), jnp.where(ok, a[i], -jnp.inf), cw[i]
        else:
            upd = jnp.logical_and(ok, a[i] > v2)
            i2, v2, w2 = jnp.where(upd, i, i2), jnp.where(upd, a[i], v2), jnp.where(upd, cw[i], w2)
    tot = w1 + w2
    e1 = best * EXP_PER_GROUP + i1
    e2 = best * EXP_PER_GROUP + i2
    lane = lax.broadcasted_iota(I32, scores.shape, 1)
    gates = jnp.where(lane == e1, w1 / tot, 0.0) + jnp.where(lane == e2, w2 / tot, 0.0)
    return gates, best


def _router_body(x_ref, w_ref, b_ref, *o_refs, d, extended):
    x = x_ref[...]
    gates, best = _route(x, w_ref[...], b_ref[...])
    if extended:
        xe_ref, grp_ref = o_refs
        xe_ref[:, :d] = x
        xe_ref[:, d:] = gates
        grp_ref[...] = best
    else:
        o_refs[0][...] = gates


def _router(x, w_router, router_bias, extended):
    n, d = x.shape
    tm = _tile(n, 512)
    if extended:
        out_specs = [pl.BlockSpec((tm, d + LANE), lambda i: (i, 0)), pl.BlockSpec((tm, 1), lambda i: (i, 0))]
        out_shape = [jax.ShapeDtypeStruct((n, d + LANE), F32), jax.ShapeDtypeStruct((n, 1), I32)]
    else:
        out_specs = pl.BlockSpec((tm, LANE), lambda i: (i, 0))
        out_shape = jax.ShapeDtypeStruct((n, LANE), F32)
    return pl.pallas_call(
        functools.partial(_router_body, d=d, extended=extended), grid=(n // tm,),
        in_specs=[pl.BlockSpec((tm, d), lambda i: (i, 0)),
                  pl.BlockSpec((d, LANE), lambda i: (0, 0)),
                  pl.BlockSpec((1, LANE), lambda i: (0, 0))],
        out_specs=out_specs, out_shape=out_shape,
        compiler_params=_cparams(("parallel",)),
        name="moe_router",
    )(x, jnp.pad(w_router, ((0, 0), (0, LANE - N_EXPERTS))),
      jnp.pad(router_bias, (0, LANE - N_EXPERTS))[None])


def _gate_column(gates, expert):
    lane = lax.broadcasted_iota(I32, gates.shape, 1)
    return jnp.sum(jnp.where(lane == expert, gates, 0.0), -1, keepdims=True)


def _experts_body(x_ref, gt_ref, wg_ref, wu_ref, wd_ref, o_ref):
    e, f = pl.program_id(1), pl.program_id(2)

    @pl.when(jnp.logical_and(e == 0, f == 0))
    def _init():
        o_ref[...] = jnp.zeros_like(o_ref)

    x = x_ref[...]
    hg = _dot(x, wg_ref[0, 0])
    hu = _dot(x, wu_ref[0, 0])
    hid = hg * jax.nn.sigmoid(hg) * hu * _gate_column(gt_ref[...], e)
    o_ref[...] += _dot(hid, wd_ref[0, 0])


def _experts_dense(x, gates, w_gate, w_up, w_down, layer):
    n, d = x.shape
    ff = w_gate.shape[-1]
    tm = _tile(n, 512)
    tf = _tile(ff, 256)
    return pl.pallas_call(
        _experts_body, grid=(n // tm, N_EXPERTS, ff // tf),
        in_specs=[pl.BlockSpec((tm, d), lambda i, e, f: (i, 0)),
                  pl.BlockSpec((tm, LANE), lambda i, e, f: (i, 0)),
                  pl.BlockSpec((1, 1, d, tf), lambda i, e, f: (layer, e, 0, f)),
                  pl.BlockSpec((1, 1, d, tf), lambda i, e, f: (layer, e, 0, f)),
                  pl.BlockSpec((1, 1, tf, d), lambda i, e, f: (layer, e, f, 0))],
        out_specs=pl.BlockSpec((tm, d), lambda i, e, f: (i, 0)),
        out_shape=jax.ShapeDtypeStruct((n, d), F32),
        compiler_params=_cparams(("parallel", "arbitrary", "arbitrary")),
        name="moe_experts_dense",
    )(x, gates, w_gate, w_up, w_down)


def _row_copy(tab_ref, buf_ref, sem, r, idx):
    return pltpu.make_async_copy(tab_ref.at[pl.ds(idx, 1)], buf_ref.at[pl.ds(r, 1)], sem.at[0])


def _gather_rows(idx_ref, tab_ref, buf_ref, sem, rows):
    base = pl.program_id(0) * rows

    def issue(r, carry):
        _row_copy(tab_ref, buf_ref, sem, r, idx_ref[base + r]).start()
        return carry

    def wait(r, carry):
        _row_copy(tab_ref, buf_ref, sem, r, 0).wait()
        return carry

    lax.fori_loop(0, rows, issue, 0)
    lax.fori_loop(0, rows, wait, 0)


def _sort_gather_body(src_ref, tab_ref, xs_ref, gs_ref, buf_ref, sem, *, rows, d):
    _gather_rows(src_ref, tab_ref, buf_ref, sem, rows)
    xs_ref[...] = buf_ref[:, :d].astype(BF16)
    gs_ref[...] = buf_ref[:, d:]


def _sort_gather(x_ext, src, rows):
    p = src.shape[0]
    d = x_ext.shape[1] - LANE
    grid_spec = pltpu.PrefetchScalarGridSpec(
        num_scalar_prefetch=1, grid=(p // rows,),
        in_specs=[pl.BlockSpec(memory_space=pl.ANY)],
        out_specs=[pl.BlockSpec((rows, d), lambda i, s: (i, 0)), pl.BlockSpec((rows, LANE), lambda i, s: (i, 0))],
        scratch_shapes=[pltpu.VMEM((rows, d + LANE), F32), pltpu.SemaphoreType.DMA((1,))])
    return pl.pallas_call(
        functools.partial(_sort_gather_body, rows=rows, d=d), grid_spec=grid_spec,
        out_shape=[jax.ShapeDtypeStruct((p, d), BF16), jax.ShapeDtypeStruct((p, LANE), F32)],
        compiler_params=_cparams(("arbitrary",)),
        name="moe_sort_gather",
    )(src, x_ext)


def _unsort_ln_body(dest_ref, ys_ref, x_ref, g_ref, b_ref, o_ref, buf_ref, sem, *, rows):
    _gather_rows(dest_ref, ys_ref, buf_ref, sem, rows)
    o_ref[...] = _ln_rows(ALPHA * x_ref[...] + buf_ref[...], g_ref[...], b_ref[...])


def _unsort_ln(y_sorted, dest, x, g, b, rows):
    n, d = x.shape
    grid_spec = pltpu.PrefetchScalarGridSpec(
        num_scalar_prefetch=1, grid=(n // rows,),
        in_specs=[pl.BlockSpec(memory_space=pl.ANY),
                  pl.BlockSpec((rows, d), lambda i, s: (i, 0)),
                  pl.BlockSpec((1, d), lambda i, s: (0, 0)),
                  pl.BlockSpec((1, d), lambda i, s: (0, 0))],
        out_specs=pl.BlockSpec((rows, d), lambda i, s: (i, 0)),
        scratch_shapes=[pltpu.VMEM((rows, d), F32), pltpu.SemaphoreType.DMA((1,))])
    return pl.pallas_call(
        functools.partial(_unsort_ln_body, rows=rows), grid_spec=grid_spec,
        out_shape=jax.ShapeDtypeStruct((n, d), F32),
        compiler_params=_cparams(("arbitrary",)),
        name="moe_unsort_ln",
    )(dest, y_sorted, x, g[None], b[None])


def _up_body(tg_ref, nv_ref, x_ref, gt_ref, wg_ref, wu_ref, h_ref):
    j, t = pl.program_id(1), pl.program_id(2)

    @pl.when(t < nv_ref[0])
    def _compute():
        x = x_ref[...]
        hg = _dot(x, wg_ref[0, 0])
        hu = _dot(x, wu_ref[0, 0])
        gate = _gate_column(gt_ref[...], tg_ref[t] * EXP_PER_GROUP + j)
        h_ref[...] = (hg * jax.nn.sigmoid(hg) * hu * gate).astype(h_ref.dtype)

    @pl.when(t >= nv_ref[0])
    def _unused_tile():
        h_ref[...] = jnp.zeros_like(h_ref)


def _experts_up(xs, gs, tile_group, n_valid, w_gate, w_up, layer, tm):
    p, d = xs.shape
    ff = w_gate.shape[-1]
    tf = _tile(ff, 512)
    nf = ff // tf
    wspec = pl.BlockSpec((1, 1, d, tf), lambda f, j, t, tg, nv: (layer, tg[t] * EXP_PER_GROUP + j, 0, f))
    grid_spec = pltpu.PrefetchScalarGridSpec(
        num_scalar_prefetch=2, grid=(nf, EXP_PER_GROUP, p // tm),
        in_specs=[pl.BlockSpec((tm, d), lambda f, j, t, tg, nv: (t, 0)),
                  pl.BlockSpec((tm, LANE), lambda f, j, t, tg, nv: (t, 0)),
                  wspec, wspec],
        out_specs=pl.BlockSpec((tm, tf), lambda f, j, t, tg, nv: (t, j * nf + f)))
    return pl.pallas_call(
        _up_body, grid_spec=grid_spec,
        out_shape=jax.ShapeDtypeStruct((p, EXP_PER_GROUP * ff), BF16),
        compiler_params=_cparams(("arbitrary", "arbitrary", "arbitrary")),
        name="moe_experts_up",
    )(tile_group, n_valid, xs, gs, w_gate, w_up)


def _down_body(tg_ref, nv_ref, h_ref, wd_ref, o_ref, *, nk):
    t, k = pl.program_id(0), pl.program_id(1)

    @pl.when(k == 0)
    def _init():
        o_ref[...] = jnp.zeros_like(o_ref)

    @pl.when(t < nv_ref[0])
    def _compute():
        o_ref[...] += _dot(h_ref[...], wd_ref[0, 0])


def _experts_down(h, tile_group, n_valid, w_down, layer, tm):
    p, width = h.shape
    ff, d = w_down.shape[-2:]
    tk = _tile(ff, 512)
    kpe = ff // tk
    nk = width // tk
    grid_spec = pltpu.PrefetchScalarGridSpec(
        num_scalar_prefetch=2, grid=(p // tm, nk),
        in_specs=[pl.BlockSpec((tm, tk), lambda t, k, tg, nv: (t, k)),
                  pl.BlockSpec((1, 1, tk, d),
                               lambda t, k, tg, nv: (layer, tg[t] * EXP_PER_GROUP + k // kpe, k % kpe, 0))],
        out_specs=pl.BlockSpec((tm, d), lambda t, k, tg, nv: (t, 0)))
    return pl.pallas_call(
        functools.partial(_down_body, nk=nk), grid_spec=grid_spec,
        out_shape=jax.ShapeDtypeStruct((p, d), F32),
        compiler_params=_cparams(("arbitrary", "arbitrary")),
        name="moe_experts_down",
    )(tile_group, n_valid, h, w_down)


def _moe_ln_grouped(x, w_router, router_bias, w_gate, w_up, w_down, layer, g, b, tm=512):
    n, d = x.shape
    x_ext, grp = _router(x, w_router, router_bias, True)
    grp = grp[:, 0]
    onehot = (grp[:, None] == jnp.arange(N_GROUPS, dtype=I32)[None]).astype(I32)
    counts = jnp.sum(onehot, 0)
    ntile = (counts + tm - 1) // tm
    tile_end = jnp.cumsum(ntile)
    tile_start = tile_end - ntile
    rank = jnp.sum(jnp.cumsum(onehot, 0) * onehot, 1) - 1
    dest = (tile_start[grp] * tm + rank).astype(I32)
    n_tiles = n // tm + N_GROUPS - 1
    src = jnp.zeros((n_tiles * tm,), I32).at[dest].set(jnp.arange(n, dtype=I32))
    n_valid = tile_end[-1:].astype(I32)
    tile_ids = jnp.minimum(jnp.arange(n_tiles, dtype=I32), n_valid[0] - 1)
    tile_group = jnp.sum((tile_ids[:, None] >= tile_end[None, :]).astype(I32), 1).astype(I32)
    xs, gs = _sort_gather(x_ext, src, 256)
    h = _experts_up(xs, gs, tile_group, n_valid, w_gate, w_up, layer, tm)
    ys = _experts_down(h, tile_group, n_valid, w_down, layer, tm)
    return _unsort_ln(ys, dest, x, g, b, 256)


def _moe_ln_dense(x, w_router, router_bias, w_gate, w_up, w_down, layer, g, b):
    gates = _router(x, w_router, router_bias, False)
    return _ln_res(x, _experts_dense(x.astype(BF16), gates, w_gate, w_up, w_down, layer), g, b)


def _project_att(x2, w_in, j):
    mm = functools.partial(_matmul, x2, w_in, j)
    wa = HA * 2 * DH
    qa = mm(col0=C_QA, ncols=wa, out_dtype=BF16)
    ka = mm(col0=C_KA, ncols=wa)
    va = mm(col0=C_VA, ncols=wa)
    qb = mm(col0=C_QB, ncols=HB * DH, out_dtype=BF16)
    kb = mm(col0=C_KB, ncols=KVB * DH)
    vb = mm(col0=C_VB, ncols=KVB * DH)
    qi = mm(col0=C_QI, ncols=H_IDX * D_IDX, out_dtype=BF16, tn=512)
    kw = mm(col0=C_KI, ncols=D_IDX + H_IDX)
    return qa, ka, va, qb, kb, vb, qi, kw[:, :D_IDX], kw[:, D_IDX:]


def _att_prompt(x, w_in, w_out, j, rel_bias, lam, lam_init, subln_g):
    b, s, d = x.shape
    x2 = x.reshape(b * s, d).astype(BF16)
    qa, ka, va, qb, kb, vb, qi, ki, wi = _project_att(x2, w_in, j)
    t = min(512, max(s // 2, 8))
    r = np.arange(t)
    dist = r[:, None] - r[None, :]
    tiles = _bias_tiles(rel_bias, np.stack([_t5_bucket_np(dist), _t5_bucket_np(dist + t)]))
    sh = lambda z: z.reshape(b, s, z.shape[-1])
    oa = _prompt_diff(sh(qa), sh(ka), sh(va), tiles[:HA], rel_bias, lam, subln_g, lam_init, t)
    mask = _prompt_select(sh(qi), sh(ki), sh(wi), min(TOPK_MAX, s // 4))
    ob = _prompt_sparse(sh(qb), sh(kb), sh(vb), mask, tiles[HA:], rel_bias, t)
    o = jnp.concatenate([oa, ob], -1).reshape(b * s, -1)
    y = _matmul(o, w_out, j)
    return y, (ka, va, kb, vb, ki)


def _att_sample(x, j, cache_ka, cache_va, cache_kb, cache_vb, cache_kidx, page_table,
                w_in, w_out, rel_bias, lam, lam_init, subln_g):
    bd, t, d = x.shape
    assert t == 8
    n_pages = page_table.shape[1]
    gp = 4 if n_pages % 4 == 0 else 1
    past_len = n_pages * PAGE
    na, n_pool = cache_ka.shape[:2]
    x2 = x.reshape(bd * t, d).astype(BF16)
    qa, ka, va, qb, kb, vb, qi, ki, wi = _project_att(x2, w_in, j)

    def pad_rows(z):
        return jnp.pad(z.reshape(bd, t, z.shape[-1]), ((0, 0), (0, PAGE - t), (0, 0)))

    tt = np.arange(t)[:, None]
    cc = np.arange(PAGE)[None, :]
    tiles = _bias_tiles(rel_bias, np.stack([_t5_bucket_np(PAGE + tt - cc), _t5_bucket_np(tt - cc)]))
    far = rel_bias[N_BUCKETS - 1]

    qa_rows = qa.reshape(bd, t, HA, 2, DH).transpose(0, 3, 2, 1, 4).reshape(bd, 2, HA * t, DH)
    ka_rows = ka.reshape(bd, t, HA, 2, DH).transpose(0, 3, 1, 2, 4).reshape(bd, 2, t * HA, DH)
    near_last = jnp.repeat(tiles[:HA, 0].reshape(HA * t, PAGE), HA, axis=1)
    near_new = jnp.repeat(tiles[:HA, 1, :, :t].reshape(HA * t, t), HA, axis=1)
    far_a = jnp.repeat(far[:HA], t)[:, None]
    oa = _sample_diff(page_table, qa_rows, cache_ka.reshape(na, n_pool, PAGE * HA * 2, DH),
                      cache_va.reshape(na, n_pool, PAGE * HA, 2 * DH), ka_rows, va.reshape(bd, t * HA, 2 * DH),
                      far_a, near_last, near_new, lam, subln_g, lam_init, j, gp)

    qi_rows = qi.reshape(bd, t, H_IDX, D_IDX).transpose(0, 2, 1, 3).reshape(bd, H_IDX * t, D_IDX)
    wi_col = wi.reshape(bd, t, H_IDX).transpose(0, 2, 1).reshape(bd, H_IDX * t, 1)
    sc = _sample_index(page_table, qi_rows, wi_col, cache_kidx, j, gp)
    mask_c, mask_n = _sample_select(sc, qi_rows, wi_col, pad_rows(ki), min(TOPK_MAX, (past_len + t) // 4))

    qb_rows = qb.reshape(bd, t, HB, DH).transpose(0, 2, 1, 3).reshape(bd, HB * t, DH)
    near_b = tiles[HA:].transpose(1, 0, 2, 3).reshape(2, HB * t, PAGE)
    far_b = jnp.repeat(far[HA:], t)[:, None]
    ob = _sample_sparse(page_table, qb_rows, cache_kb.reshape(na, n_pool, PAGE * KVB, DH),
                        cache_vb.reshape(na, n_pool, PAGE * KVB, DH), pad_rows(kb), pad_rows(vb),
                        far_b, near_b, mask_c, mask_n, j, gp)
    ob = ob.reshape(bd, HB, t, DH).transpose(0, 2, 1, 3).reshape(bd, t, HB * DH)
    o = jnp.concatenate([oa, ob], -1).reshape(bd * t, -1).astype(BF16)
    y = _matmul(o, w_out, j)
    return y, (ka, va, kb, vb, ki)


def _softplus(z):
    return jnp.maximum(z, 0.0) + jnp.log(1.0 + jnp.exp(-jnp.abs(z)))


def _rwkv(x, shift0, wkv0, j, p, *, chunk, heads, exact):
    b, t, d = x.shape
    xr, xw, xk, xv, xa, xg = [z.reshape(b * t, d) for z in _rw_prep(x, shift0, p['mix'][j])]
    r = _matmul(xr, p['wr'], j)
    k = _matmul(xk, p['wk'], j)
    v = _matmul(xv, p['wv'], j)
    w = _matmul(_matmul(xw, p['w1'], j, epilogue=jnp.tanh), p['w2'], j,
                epilogue=lambda acc, w0: -_softplus(-(w0 + acc)) - 0.5, extras=(p['w0'][j][None],))
    a = _matmul(_matmul(xa, p['a1'], j), p['a2'], j,
                epilogue=lambda acc, a0: jax.nn.sigmoid(a0 + acc), extras=(p['a0'][j][None],))
    g = _matmul(_matmul(xg, p['g1'], j, epilogue=jax.nn.sigmoid), p['g2'], j)
    sh = lambda z: z.reshape(b, t, d)
    y_pre, s_fin = _rw_scan(sh(r), sh(w), sh(k), sh(v), sh(a), sh(g), p['kk'][j], p['ka'][j], p['rk'][j],
                            p['gn_g'][j], p['gn_b'][j], wkv0, chunk=chunk, heads=heads, exact=exact)
    y = _matmul(y_pre.reshape(b * t, d), p['wo'], j)
    return y, s_fin, x[:, -1]


def kernel(x_prompt, x_sample, cache_ka, cache_va, cache_kb, cache_vb, cache_kidx, state_wkv, state_shift,
           page_table, rel_bias, w_in_att, w_out_att, lam_q1, lam_k1, lam_q2, lam_k2, subln_g,
           rw_mix, rw_w0, rw_w1, rw_w2, rw_a0, rw_a1, rw_a2, rw_g1, rw_g2, rw_kk, rw_ka, rw_rk,
           rw_wr, rw_wk, rw_wv, rw_wo, rw_gn_g, rw_gn_b, w_router, router_bias, w_gate, w_up, w_down,
           ln1_g, ln1_b, ln2_g, ln2_b):
    bp, s, d = x_prompt.shape
    bd, t, _ = x_sample.shape
    rw = dict(mix=rw_mix, w0=rw_w0, w1=rw_w1, w2=rw_w2, a0=rw_a0, a1=rw_a1, a2=rw_a2, g1=rw_g1, g2=rw_g2,
              kk=rw_kk, ka=rw_ka, rk=rw_rk, wr=rw_wr, wk=rw_wk, wv=rw_wv, wo=rw_wo, gn_g=rw_gn_g, gn_b=rw_gn_b)
    moe_w = (w_router, router_bias, w_gate, w_up, w_down)
    yp, ys = x_prompt, x_sample
    rows_p, rows_s = [], []
    wkv_p, shift_p, wkv_s, shift_s = [], [], [], []
    n_layers = w_gate.shape[0]
    for i in range(n_layers):
        j = i // 2
        if i % 2 == 0:
            lam_init = 0.8 - 0.6 * math.exp(-0.3 * i)
            lam = jnp.stack([lam_q1[j], lam_k1[j], lam_q2[j], lam_k2[j]], 0)
            mp, rp = _att_prompt(yp, w_in_att, w_out_att, j, rel_bias, lam, lam_init, subln_g[j][None])
            ms, rs = _att_sample(ys, j, cache_ka, cache_va, cache_kb, cache_vb, cache_kidx, page_table,
                                 w_in_att, w_out_att, rel_bias, lam, lam_init, subln_g[j][None])
            rows_p.append(rp)
            rows_s.append(rs)
        else:
            nh = d // RW_N
            mp, sp, hp = _rwkv(yp, jnp.zeros((bp, d), F32), jnp.zeros((bp, nh, RW_N, RW_N), F32), j, rw,
                               chunk=64, heads=8, exact=False)
            ms, ss, hs = _rwkv(ys, state_shift[j], state_wkv[j], j, rw, chunk=8, heads=8, exact=True)
            wkv_p.append(sp)
            shift_p.append(hp)
            wkv_s.append(ss)
            shift_s.append(hs)
        yp2 = _ln_res(yp.reshape(bp * s, d), mp, ln1_g[i], ln1_b[i])
        ys2 = _ln_res(ys.reshape(bd * t, d), ms, ln1_g[i], ln1_b[i])
        yp2 = _moe_ln_grouped(yp2, *moe_w, i, ln2_g[i], ln2_b[i])
        ys2 = _moe_ln_dense(ys2, *moe_w, i, ln2_g[i], ln2_b[i])
        yp, ys = yp2.reshape(bp, s, d), ys2.reshape(bd, t, d)

    def rows(rs_list, b_, t_):
        ka, va, kb, vb, ki = (jnp.stack(z, 0) for z in zip(*rs_list))
        n = ka.shape[0]
        return (ka.reshape(n, b_, t_, HA, 2, DH), va.reshape(n, b_, t_, HA, 2 * DH),
                kb.reshape(n, b_, t_, KVB, DH), vb.reshape(n, b_, t_, KVB, DH), ki.reshape(n, b_, t_, D_IDX))

    return (yp, ys) + rows(rows_p, bp, s) + rows(rows_s, bd, t) + (
        jnp.stack(wkv_p, 0), jnp.stack(shift_p, 0), jnp.stack(wkv_s, 0), jnp.stack(shift_s, 0))
```

```python
import functools
import math

import numpy as np
import jax
import jax.numpy as jnp
from jax import lax
from jax.experimental import pallas as pl
from jax.experimental.pallas import tpu as pltpu

F32 = jnp.float32
BF16 = jnp.bfloat16
I32 = jnp.int32
HIGHEST = lax.Precision.HIGHEST

HA, DH = 8, 128
HB, KVB = 16, 2
GB = HB // KVB
H_IDX, D_IDX = 32, 128
TOPK_MAX = 256
IDX_SCALE = (H_IDX * D_IDX) ** -0.5
N_BUCKETS, BUCKET_EXACT, BUCKET_MAX_DIST = 32, 16, 128
RW_N = 64
RW_GN_EPS = 64e-5
N_EXPERTS, N_GROUPS = 16, 4
EXP_PER_GROUP = N_EXPERTS // N_GROUPS
LN_EPS = 1e-5
DEPTH = 2
ALPHA = (2 * DEPTH) ** 0.25
PAGE = 128
ATT_SCALE = DH ** -0.5
MASKED = -1e30

C_QA, C_KA, C_VA = 0, HA * 2 * DH, 2 * HA * 2 * DH
C_QB = 3 * HA * 2 * DH
C_KB = C_QB + HB * DH
C_VB = C_KB + KVB * DH
C_QI = C_VB + KVB * DH
C_KI = C_QI + H_IDX * D_IDX
C_WI = C_KI + D_IDX
IN_WIDTH = C_WI + H_IDX

LANE = 128
VMEM_LIMIT = 56 * 1024 * 1024


def _cparams(sem, vmem=VMEM_LIMIT):
    return pltpu.CompilerParams(dimension_semantics=sem, vmem_limit_bytes=vmem)


def _tile(dim, pref):
    return dim if dim <= pref else pref


def _dot(a, b):
    return jnp.dot(a.astype(BF16), b.astype(BF16), preferred_element_type=F32)


def _dot_nt(a, b):
    return lax.dot_general(a.astype(BF16), b.astype(BF16), (((1,), (1,)), ((), ())),
                           preferred_element_type=F32)


def _mm_body(*refs, nk, n_extra, epilogue, ow):
    x_ref, w_ref = refs[0], refs[1]
    extras = refs[2:2 + n_extra]
    o_ref, acc_ref = refs[2 + n_extra], refs[3 + n_extra]
    k = pl.program_id(2)

    @pl.when(k == 0)
    def _init():
        acc_ref[...] = jnp.zeros_like(acc_ref)

    acc_ref[...] += _dot(x_ref[...], w_ref[...])

    @pl.when(k == nk - 1)
    def _finish():
        acc = acc_ref[...][:, :ow]
        if epilogue is not None:
            acc = epilogue(acc, *[e[...] for e in extras])
        o_ref[...] = acc.astype(o_ref.dtype)


def _matmul(x, w, layer, *, col0=0, ncols=None, out_dtype=F32, epilogue=None, extras=(),
            tm=1024, tn=1024, tk=2048):
    m, kdim = x.shape
    nfull = w.shape[-1]
    ncols = nfull if ncols is None else ncols
    tm = _tile(m, tm)
    tk = _tile(kdim, tk)
    assert m % tm == 0 and kdim % tk == 0
    if ncols <= tn:
        if col0 == 0 and ncols == nfull:
            tnw = ncols
        else:
            tnw = -(-ncols // LANE) * LANE
        tno = ncols
    else:
        tnw = tno = tn
    assert col0 % tnw == 0
    cb = col0 // tnw
    nk = kdim // tk
    grid = (m // tm, pl.cdiv(ncols, tno), nk)
    in_specs = [pl.BlockSpec((tm, tk), lambda i, j, k: (i, k)),
                pl.BlockSpec((None, tk, tnw), lambda i, j, k: (layer, k, j + cb))]
    for _ in extras:
        in_specs.append(pl.BlockSpec((1, tno), lambda i, j, k: (0, j)))
    return pl.pallas_call(
        functools.partial(_mm_body, nk=nk, n_extra=len(extras), epilogue=epilogue, ow=tno),
        grid=grid,
        in_specs=in_specs,
        out_specs=pl.BlockSpec((tm, tno), lambda i, j, k: (i, j)),
        out_shape=jax.ShapeDtypeStruct((m, ncols), out_dtype),
        scratch_shapes=[pltpu.VMEM((tm, tnw), F32)],
        compiler_params=_cparams(("parallel", "parallel", "arbitrary")),
        name="matmul",
    )(x, w, *extras)


def _ln_rows(z, g, b):
    mu = jnp.mean(z, -1, keepdims=True)
    zc = z - mu
    var = jnp.mean(zc * zc, -1, keepdims=True)
    return zc * lax.rsqrt(var + LN_EPS) * g + b


def _ln_body(x_ref, m_ref, g_ref, b_ref, o_ref):
    o_ref[...] = _ln_rows(ALPHA * x_ref[...] + m_ref[...], g_ref[...], b_ref[...])


def _ln_res(x, m, g, b):
    n, d = x.shape
    tm = _tile(n, 256)
    row = pl.BlockSpec((tm, d), lambda i: (i, 0))
    vec = pl.BlockSpec((1, d), lambda i: (0, 0))
    return pl.pallas_call(
        _ln_body, grid=(n // tm,), in_specs=[row, row, vec, vec], out_specs=row,
        out_shape=jax.ShapeDtypeStruct((n, d), F32),
        compiler_params=_cparams(("parallel",)),
        name="ln_residual",
    )(x, m, g[None], b[None])


def _t5_bucket_np(dist):
    n = np.maximum(dist, 0)
    nf = np.maximum(n, 1).astype(np.float32)
    large = BUCKET_EXACT + (np.log(nf / np.float32(BUCKET_EXACT)) / np.float32(math.log(BUCKET_MAX_DIST / BUCKET_EXACT))
                            * np.float32(N_BUCKETS - BUCKET_EXACT)).astype(np.int32)
    large = np.minimum(large, N_BUCKETS - 1)
    return np.where(n < BUCKET_EXACT, n, large).astype(np.int32)


def _bias_body(rel_ref, bkt_ref, o_ref):
    h = pl.program_id(0)
    bkt = bkt_ref[...]
    out = jnp.zeros(bkt.shape, F32)
    for b in range(N_BUCKETS):
        out = jnp.where(bkt == b, rel_ref[b, h], out)
    o_ref[0] = out


def _bias_tiles(rel_bias, buckets):
    nh = rel_bias.shape[1]
    n, r, c = buckets.shape
    return pl.pallas_call(
        _bias_body, grid=(nh,),
        in_specs=[pl.BlockSpec(memory_space=pltpu.SMEM),
                  pl.BlockSpec((n, r, c), lambda h: (0, 0, 0))],
        out_specs=pl.BlockSpec((1, n, r, c), lambda h: (h, 0, 0, 0)),
        out_shape=jax.ShapeDtypeStruct((nh, n, r, c), F32),
        compiler_params=_cparams(("arbitrary",)),
        name="bias_tiles",
    )(rel_bias, jnp.asarray(buckets))


def _softmax_step(s, valid, m_old, l_old):
    if valid is not None:
        s = jnp.where(valid, s, MASKED)
    m_new = jnp.maximum(m_old, jnp.max(s, -1, keepdims=True))
    p = jnp.exp(s - m_new)
    if valid is not None:
        p = jnp.where(valid, p, 0.0)
    alpha = jnp.exp(m_old - m_new)
    l_new = alpha * l_old + jnp.sum(p, -1, keepdims=True)
    return p, alpha, m_new, l_new


def _diff_finish(o1, o2, lv, g, lam_init):
    lam = (jnp.exp(jnp.sum(lv[0:1] * lv[1:2], -1, keepdims=True))
           - jnp.exp(jnp.sum(lv[2:3] * lv[3:4], -1, keepdims=True)) + lam_init)
    o = o1 - lam * o2
    o = o * lax.rsqrt(jnp.mean(o * o, -1, keepdims=True) + 1e-5)
    return o * g * (1.0 - lam_init)


def _tri_tables(nq):
    iq = np.concatenate([np.full(i + 1, i) for i in range(nq)])
    ik = np.concatenate([np.arange(i + 1) for i in range(nq)])
    return jnp.asarray(iq, I32), jnp.asarray(ik, I32)


def _pdiff_body(iq_ref, ik_ref, rel_ref, lam_ref, q_ref, k_ref, v_ref, bias_ref, g_ref, o_ref,
                m_ref, l_ref, acc_ref, *, t, lam_init):
    h, st = pl.program_id(1), pl.program_id(2)
    iq, ik = iq_ref[st], ik_ref[st]

    @pl.when(ik == 0)
    def _init():
        m_ref[...] = jnp.full(m_ref.shape, MASKED, F32)
        l_ref[...] = jnp.zeros_like(l_ref)
        acc_ref[...] = jnp.zeros_like(acc_ref)

    def step(bias, valid):
        q = q_ref[0]
        k = k_ref[0]
        v = v_ref[0].astype(BF16)
        for c in range(2):
            s = _dot_nt(q[:, c * DH:(c + 1) * DH], k[:, c * DH:(c + 1) * DH]) * ATT_SCALE + bias
            p, alpha, m_new, l_new = _softmax_step(s, valid, m_ref[c], l_ref[c])
            acc_ref[c] = alpha * acc_ref[c] + jnp.dot(p.astype(BF16), v, preferred_element_type=F32)
            m_ref[c] = m_new
            l_ref[c] = l_new

    @pl.when(ik == iq)
    def _diagonal():
        row = lax.broadcasted_iota(I32, (t, t), 0)
        col = lax.broadcasted_iota(I32, (t, t), 1)
        step(bias_ref[0, 0], col <= row)

    @pl.when(ik == iq - 1)
    def _near():
        step(bias_ref[0, 0], None)

    @pl.when(ik < iq - 1)
    def _far():
        step(rel_ref[N_BUCKETS - 1, h], None)

    @pl.when(ik == iq)
    def _finish():
        o = _diff_finish(acc_ref[0] / l_ref[0], acc_ref[1] / l_ref[1], lam_ref[...], g_ref[...], lam_init)
        o_ref[0] = o.astype(o_ref.dtype)


def _prompt_diff(qa, ka, va, bias_a, rel_bias, lam, subln_g, lam_init, t):
    b, s, _ = qa.shape
    iq_tab, ik_tab = _tri_tables(s // t)
    wv = 2 * DH
    grid_spec = pltpu.PrefetchScalarGridSpec(
        num_scalar_prefetch=2, grid=(b, HA, iq_tab.shape[0]),
        in_specs=[pl.BlockSpec(memory_space=pltpu.SMEM),
                  pl.BlockSpec((4, DH), lambda b_, h, st, iq, ik: (0, 0)),
                  pl.BlockSpec((1, t, wv), lambda b_, h, st, iq, ik: (b_, iq[st], h)),
                  pl.BlockSpec((1, t, wv), lambda b_, h, st, iq, ik: (b_, ik[st], h)),
                  pl.BlockSpec((1, t, wv), lambda b_, h, st, iq, ik: (b_, ik[st], h)),
                  pl.BlockSpec((1, 1, t, t), lambda b_, h, st, iq, ik: (h, jnp.where(ik[st] == iq[st], 0, 1), 0, 0)),
                  pl.BlockSpec((1, wv), lambda b_, h, st, iq, ik: (0, 0))],
        out_specs=pl.BlockSpec((1, t, wv), lambda b_, h, st, iq, ik: (b_, iq[st], h)),
        scratch_shapes=[pltpu.VMEM((2, t, 1), F32), pltpu.VMEM((2, t, 1), F32), pltpu.VMEM((2, t, wv), F32)])
    return pl.pallas_call(
        functools.partial(_pdiff_body, t=t, lam_init=lam_init),
        grid_spec=grid_spec,
        out_shape=jax.ShapeDtypeStruct((b, s, HA * wv), BF16),
        compiler_params=_cparams(("parallel", "parallel", "arbitrary")),
        name="prompt_diff_attention",
    )(iq_tab, ik_tab, rel_bias, lam, qa, ka, va, bias_a, subln_g)


def _count(cond):
    return jnp.sum(jnp.where(cond, 1.0, 0.0), -1, keepdims=True)


def _topk_mask(sc, k, n_idx_bits):
    bits = lax.bitcast_convert_type(sc, I32)
    key = bits ^ ((bits >> 31) & jnp.int32(0x7FFFFFFF))
    kf = jnp.float32(k)
    int_min = jnp.int32(-2 ** 31)
    thr = jnp.where(_count(key >= 0) >= kf, jnp.int32(0), int_min)
    for bit in range(30, -1, -1):
        cand = thr | jnp.int32(1 << bit)
        thr = jnp.where(_count(key >= cand) >= kf, cand, thr)
    above = key > thr
    tie = key == thr
    need = kf - _count(above)
    idx = lax.broadcasted_iota(I32, sc.shape, sc.ndim - 1)
    last = jnp.zeros_like(thr)
    for bit in range(n_idx_bits - 1, -1, -1):
        cand = last | jnp.int32(1 << bit)
        last = jnp.where(_count(jnp.logical_and(tie, idx < cand)) < need, cand, last)
    return jnp.logical_or(above, jnp.logical_and(tie, idx <= last))


def _pindex_body(qi_ref, ki_ref, wi_ref, o_ref, *, tq, s, topk, nseg):
    iq = pl.program_id(1)
    per_seg = s // tq // nseg

    def select(nkeys):
        ki = ki_ref[0, :nkeys, :].astype(BF16)
        wi = wi_ref[0]
        sc = jnp.zeros((tq, nkeys), F32)
        for h in range(H_IDX):
            d = lax.dot_general(qi_ref[0, :, h * D_IDX:(h + 1) * D_IDX], ki, (((1,), (1,)), ((), ())),
                                preferred_element_type=F32)
            sc = sc + jnp.maximum(d, 0.0) * wi[:, h:h + 1]
        sc = sc * IDX_SCALE
        qpos = iq * tq + lax.broadcasted_iota(I32, (tq, nkeys), 0)
        kpos = lax.broadcasted_iota(I32, (tq, nkeys), 1)
        causal = kpos <= qpos
        sc = jnp.where(causal, sc, -jnp.inf)
        sel = jnp.logical_and(_topk_mask(sc, topk, int(nkeys).bit_length()), causal)
        o_ref[0, :, :nkeys] = jnp.where(sel, 1.0, 0.0)
        if nkeys < s:
            o_ref[0, :, nkeys:] = jnp.zeros((tq, s - nkeys), F32)

    for seg in range(nseg):
        pl.when(iq // per_seg == seg)(functools.partial(select, (seg + 1) * per_seg * tq))


def _prompt_select(qi, ki, wi, topk):
    b, s, _ = qi.shape
    tq = _tile(s, 128)
    nseg = 4 if s % (4 * tq) == 0 else 1
    return pl.pallas_call(
        functools.partial(_pindex_body, tq=tq, s=s, topk=topk, nseg=nseg),
        grid=(b, s // tq),
        in_specs=[pl.BlockSpec((1, tq, H_IDX * D_IDX), lambda b_, i: (b_, i, 0)),
                  pl.BlockSpec((1, s, D_IDX), lambda b_, i: (b_, 0, 0)),
                  pl.BlockSpec((1, tq, H_IDX), lambda b_, i: (b_, i, 0))],
        out_specs=pl.BlockSpec((1, tq, s), lambda b_, i: (b_, i, 0)),
        out_shape=jax.ShapeDtypeStruct((b, s, s), F32),
        compiler_params=_cparams(("parallel", "parallel")),
        name="prompt_indexer_topk_mask",
    )(qi, ki, wi)


def _psparse_body(iq_ref, ik_ref, rel_ref, q_ref, k_ref, v_ref, msk_ref, bias_ref, o_ref,
                  m_ref, l_ref, acc_ref):
    g, st = pl.program_id(1), pl.program_id(2)
    iq, ik = iq_ref[st], ik_ref[st]

    @pl.when(ik == 0)
    def _init():
        m_ref[...] = jnp.full(m_ref.shape, MASKED, F32)
        l_ref[...] = jnp.zeros_like(l_ref)
        acc_ref[...] = jnp.zeros_like(acc_ref)

    def step(bias_of):
        k = k_ref[0].astype(BF16)
        v = v_ref[0].astype(BF16)
        valid = msk_ref[0] > 0.5
        for r in range(GB):
            s = _dot_nt(q_ref[0, :, r * DH:(r + 1) * DH], k) * ATT_SCALE + bias_of(r)
            p, alpha, m_new, l_new = _softmax_step(s, valid, m_ref[r], l_ref[r])
            acc_ref[r] = alpha * acc_ref[r] + jnp.dot(p.astype(BF16), v, preferred_element_type=F32)
            m_ref[r] = m_new
            l_ref[r] = l_new

    @pl.when(ik >= iq - 1)
    def _near():
        step(lambda r: bias_ref[r, 0])

    @pl.when(ik < iq - 1)
    def _far():
        step(lambda r: rel_ref[N_BUCKETS - 1, HA + g * GB + r])

    @pl.when(ik == iq)
    def _finish():
        for r in range(GB):
            o_ref[0, :, r * DH:(r + 1) * DH] = (acc_ref[r] / l_ref[r]).astype(o_ref.dtype)


def _prompt_sparse(qb, kb, vb, mask, bias_b, rel_bias, t):
    b, s, _ = qb.shape
    iq_tab, ik_tab = _tri_tables(s // t)
    grid_spec = pltpu.PrefetchScalarGridSpec(
        num_scalar_prefetch=2, grid=(b, KVB, iq_tab.shape[0]),
        in_specs=[pl.BlockSpec(memory_space=pltpu.SMEM),
                  pl.BlockSpec((1, t, GB * DH), lambda b_, g, st, iq, ik: (b_, iq[st], g)),
                  pl.BlockSpec((1, t, DH), lambda b_, g, st, iq, ik: (b_, ik[st], g)),
                  pl.BlockSpec((1, t, DH), lambda b_, g, st, iq, ik: (b_, ik[st], g)),
                  pl.BlockSpec((1, t, t), lambda b_, g, st, iq, ik: (b_, iq[st], ik[st])),
                  pl.BlockSpec((GB, 1, t, t),
                               lambda b_, g, st, iq, ik: (g, jnp.where(ik[st] == iq[st], 0, 1), 0, 0))],
        out_specs=pl.BlockSpec((1, t, GB * DH), lambda b_, g, st, iq, ik: (b_, iq[st], g)),
        scratch_shapes=[pltpu.VMEM((GB, t, 1), F32), pltpu.VMEM((GB, t, 1), F32), pltpu.VMEM((GB, t, DH), F32)])
    return pl.pallas_call(
        _psparse_body, grid_spec=grid_spec,
        out_shape=jax.ShapeDtypeStruct((b, s, HB * DH), BF16),
        compiler_params=_cparams(("parallel", "parallel", "arbitrary")),
        name="prompt_sparse_attention",
    )(iq_tab, ik_tab, rel_bias, qb, kb, vb, mask, bias_b)


def _page_cat(ref, nsub):
    parts = [ref[0, 0, pl.ds(i, PAGE, stride=nsub), :].astype(BF16) for i in range(nsub)]
    return parts[0] if nsub == 1 else jnp.concatenate(parts, 1)


def _index_scores(q, w, keys):
    r = jnp.maximum(_dot_nt(q, keys), 0.0) * w
    return jnp.sum(r.reshape(H_IDX, 8, keys.shape[0]), 0) * IDX_SCALE


def _sindex_body(pt_ref, q_ref, w_ref, *refs, gp):
    kc_refs, o_ref = refs[:gp], refs[gp]
    keys = jnp.concatenate([_page_cat(r, 1) for r in kc_refs], 0)
    o_ref[0] = _index_scores(q_ref[0], w_ref[0], keys)


def _sample_index(page_table, qi_rows, wi_col, cache_kidx, layer, gp):
    bd, n_pages = page_table.shape
    rows = qi_rows.shape[1]

    def page(i):
        return lambda b, p, pt: (layer, pt[b, p * gp + i], 0, 0)

    grid_spec = pltpu.PrefetchScalarGridSpec(
        num_scalar_prefetch=1, grid=(bd, n_pages // gp),
        in_specs=[pl.BlockSpec((1, rows, D_IDX), lambda b, p, pt: (b, 0, 0)),
                  pl.BlockSpec((1, rows, 1), lambda b, p, pt: (b, 0, 0))]
        + [pl.BlockSpec((1, 1, PAGE, D_IDX), page(i)) for i in range(gp)],
        out_specs=pl.BlockSpec((1, 8, gp * PAGE), lambda b, p, pt: (b, 0, p)))
    return pl.pallas_call(
        functools.partial(_sindex_body, gp=gp), grid_spec=grid_spec,
        out_shape=jax.ShapeDtypeStruct((bd, 8, n_pages * PAGE), F32),
        compiler_params=_cparams(("parallel", "arbitrary")),
        name="sample_indexer_scores",
    )(page_table, qi_rows, wi_col, *([cache_kidx] * gp))


def _sselect_body(sc_ref, q_ref, w_ref, kn_ref, oc_ref, on_ref, *, past_len, topk, nbits):
    sc_new = _index_scores(q_ref[0], w_ref[0], kn_ref[0].astype(BF16))
    sc = jnp.concatenate([sc_ref[0], sc_new], 1)
    qpos = past_len + lax.broadcasted_iota(I32, sc.shape, 0)
    kpos = lax.broadcasted_iota(I32, sc.shape, 1)
    causal = kpos <= qpos
    sc = jnp.where(causal, sc, -jnp.inf)
    sel = jnp.where(jnp.logical_and(_topk_mask(sc, topk, nbits), causal), 1.0, 0.0)
    oc_ref[0] = sel[:, :past_len]
    on_ref[0] = sel[:, past_len:]


def _sample_select(sc, qi_rows, wi_col, ki_new, topk):
    bd, t, past_len = sc.shape
    rows = qi_rows.shape[1]
    per_b = lambda b: (b, 0, 0)
    return pl.pallas_call(
        functools.partial(_sselect_body, past_len=past_len, topk=topk,
                          nbits=int(past_len + PAGE).bit_length()),
        grid=(bd,),
        in_specs=[pl.BlockSpec((1, t, past_len), per_b), pl.BlockSpec((1, rows, D_IDX), per_b),
                  pl.BlockSpec((1, rows, 1), per_b), pl.BlockSpec((1, PAGE, D_IDX), per_b)],
        out_specs=[pl.BlockSpec((1, t, past_len), per_b), pl.BlockSpec((1, t, PAGE), per_b)],
        out_shape=[jax.ShapeDtypeStruct((bd, t, past_len), F32), jax.ShapeDtypeStruct((bd, t, PAGE), F32)],
        compiler_params=_cparams(("parallel",)),
        name="sample_topk_mask",
    )(sc, qi_rows, wi_col, ki_new)


def _ssparse_body(pt_ref, q_ref, *refs, n_steps, gp):
    kc_refs, vc_refs = refs[:gp], refs[gp:2 * gp]
    kn_ref, vn_ref, far_ref, near_ref, mc_ref, mn_ref, o_ref, m_ref, l_ref, acc_ref = refs[2 * gp:]
    p = pl.program_id(1)

    @pl.when(p == 0)
    def _init():
        m_ref[...] = jnp.full(m_ref.shape, MASKED, F32)
        l_ref[...] = jnp.zeros_like(l_ref)
        acc_ref[...] = jnp.zeros_like(acc_ref)

    q = q_ref[0]
    rows = q.shape[0]
    half = rows // KVB

    def process(k, v, bias, valid):
        s = jnp.concatenate([_dot_nt(q[g * half:(g + 1) * half], k[:, g * DH:(g + 1) * DH])
                             for g in range(KVB)], 0)
        s = s * ATT_SCALE + bias
        pr, alpha, m_new, l_new = _softmax_step(s, valid, m_ref[...], l_ref[...])
        pr = pr.astype(BF16)
        pv = jnp.concatenate([jnp.dot(pr[g * half:(g + 1) * half], v[:, g * DH:(g + 1) * DH],
                                      preferred_element_type=F32) for g in range(KVB)], 0)
        acc_ref[...] = alpha * acc_ref[...] + pv
        m_ref[...] = m_new
        l_ref[...] = l_new

    def row_mask(m):
        return jnp.concatenate([m] * (rows // 8), 0) > 0.5

    @pl.when(p < n_steps)
    def _cache():
        k = jnp.concatenate([_page_cat(r, KVB) for r in kc_refs], 0)
        v = jnp.concatenate([_page_cat(r, KVB) for r in vc_refs], 0)
        far = jnp.broadcast_to(far_ref[...], (rows, PAGE))
        last = jnp.where(p == n_steps - 1, near_ref[0], far)
        bias = jnp.concatenate([far] * (gp - 1) + [last], 1)
        process(k, v, bias, row_mask(mc_ref[0]))

    @pl.when(p == n_steps)
    def _new():
        tok = lax.broadcasted_iota(I32, (rows, PAGE), 0) % 8
        col = lax.broadcasted_iota(I32, (rows, PAGE), 1)
        valid = jnp.logical_and(col <= tok, row_mask(mn_ref[0]))
        process(kn_ref[0].astype(BF16), vn_ref[0].astype(BF16), near_ref[1], valid)
        o_ref[0] = acc_ref[...] / l_ref[...]


def _sample_sparse(page_table, q_rows, kcache, vcache, knew, vnew, far_col, near, mask_c, mask_n, layer, gp):
    bd, n_pages = page_table.shape
    n_steps = n_pages // gp
    rows = q_rows.shape[1]

    def page(i):
        return lambda b, p, pt: (layer, pt[b, jnp.minimum(p, n_steps - 1) * gp + i], 0, 0)

    per_b = lambda b, p, pt: (b, 0, 0)
    cache_spec = [pl.BlockSpec((1, 1, PAGE * KVB, DH), page(i)) for i in range(gp)]
    in_specs = ([pl.BlockSpec((1, rows, DH), per_b)] + cache_spec + cache_spec
                + [pl.BlockSpec((1, PAGE, KVB * DH), per_b),
                   pl.BlockSpec((1, PAGE, KVB * DH), per_b),
                   pl.BlockSpec((rows, 1), lambda b, p, pt: (0, 0)),
                   pl.BlockSpec((2, rows, PAGE), lambda b, p, pt: (0, 0, 0)),
                   pl.BlockSpec((1, 8, gp * PAGE), lambda b, p, pt: (b, 0, jnp.minimum(p, n_steps - 1))),
                   pl.BlockSpec((1, 8, PAGE), per_b)])
    grid_spec = pltpu.PrefetchScalarGridSpec(
        num_scalar_prefetch=1, grid=(bd, n_steps + 1), in_specs=in_specs,
        out_specs=pl.BlockSpec((1, rows, DH), per_b),
        scratch_shapes=[pltpu.VMEM((rows, 1), F32), pltpu.VMEM((rows, 1), F32), pltpu.VMEM((rows, DH), F32)])
    return pl.pallas_call(
        functools.partial(_ssparse_body, n_steps=n_steps, gp=gp),
        grid_spec=grid_spec, out_shape=jax.ShapeDtypeStruct((bd, rows, DH), F32),
        compiler_params=_cparams(("parallel", "arbitrary")),
        name="sample_sparse_attention",
    )(page_table, q_rows, *([kcache] * gp), *([vcache] * gp), knew, vnew, far_col, near, mask_c, mask_n)


def _sdiff_body(pt_ref, lam_ref, q_ref, *refs, n_steps, gp, lam_init):
    kc_refs, vc_refs = refs[:gp], refs[gp:2 * gp]
    kn_ref, vn_ref, far_ref, near_ref, nearn_ref, g_ref, o_ref, m_ref, l_ref, acc_ref = refs[2 * gp:]
    p = pl.program_id(1)
    rows = HA * 8

    @pl.when(p == 0)
    def _init():
        m_ref[...] = jnp.full(m_ref.shape, MASKED, F32)
        l_ref[...] = jnp.zeros_like(l_ref)
        acc_ref[...] = jnp.zeros_like(acc_ref)

    def process(c, k, v, bias, valid):
        s = _dot_nt(q_ref[0, c], k) * ATT_SCALE + bias
        pr, alpha, m_new, l_new = _softmax_step(s, valid, m_ref[c], l_ref[c])
        acc_ref[c] = alpha * acc_ref[c] + jnp.dot(pr.astype(BF16), v, preferred_element_type=F32)
        m_ref[c] = m_new
        l_ref[c] = l_new

    def own_head(ncol):
        row = lax.broadcasted_iota(I32, (rows, ncol), 0)
        col = lax.broadcasted_iota(I32, (rows, ncol), 1)
        return (col % HA) == (row // 8), row, col

    @pl.when(p < n_steps)
    def _cache():
        valid, _, _ = own_head(gp * PAGE * HA)
        v = jnp.concatenate([r[0, 0].astype(BF16) for r in vc_refs], 0)
        far = jnp.broadcast_to(far_ref[...], (rows, PAGE * HA))
        last = jnp.where(p == n_steps - 1, near_ref[...], far)
        bias = jnp.concatenate([far] * (gp - 1) + [last], 1)
        for c in range(2):
            k = jnp.concatenate([r[0, 0, pl.ds(c, PAGE * HA, stride=2), :].astype(BF16) for r in kc_refs], 0)
            process(c, k, v, bias, valid)

    @pl.when(p == n_steps)
    def _new():
        valid, row, col = own_head(8 * HA)
        valid = jnp.logical_and(valid, col // HA <= row % 8)
        v = vn_ref[0].astype(BF16)
        for c in range(2):
            process(c, kn_ref[0, c].astype(BF16), v, nearn_ref[...], valid)
        wv = 2 * DH
        o1 = acc_ref[0] / l_ref[0]
        o2 = acc_ref[1] / l_ref[1]
        for h in range(HA):
            o_ref[0, :, h * wv:(h + 1) * wv] = _diff_finish(o1[h * 8:(h + 1) * 8], o2[h * 8:(h + 1) * 8],
                                                            lam_ref[...], g_ref[...], lam_init)


def _sample_diff(page_table, q_rows, kcache, vcache, knew, vnew, far_col, near_last, near_new, lam, subln_g,
                 lam_init, layer, gp):
    bd, n_pages = page_table.shape
    n_steps = n_pages // gp
    rows = HA * 8
    wv = 2 * DH

    def page(i):
        return lambda b, p, pt: (layer, pt[b, jnp.minimum(p, n_steps - 1) * gp + i], 0, 0)

    per_b3 = lambda b, p, pt: (b, 0, 0)
    per_b4 = lambda b, p, pt: (b, 0, 0, 0)
    const2 = lambda b, p, pt: (0, 0)
    in_specs = ([pl.BlockSpec((4, DH), const2), pl.BlockSpec((1, 2, rows, DH), per_b4)]
                + [pl.BlockSpec((1, 1, PAGE * HA * 2, DH), page(i)) for i in range(gp)]
                + [pl.BlockSpec((1, 1, PAGE * HA, wv), page(i)) for i in range(gp)]
                + [pl.BlockSpec((1, 2, rows, DH), per_b4),
                   pl.BlockSpec((1, rows, wv), per_b3),
                   pl.BlockSpec((rows, 1), const2),
                   pl.BlockSpec((rows, PAGE * HA), const2),
                   pl.BlockSpec((rows, 8 * HA), const2),
                   pl.BlockSpec((1, wv), const2)])
    grid_spec = pltpu.PrefetchScalarGridSpec(
        num_scalar_prefetch=1, grid=(bd, n_steps + 1), in_specs=in_specs,
        out_specs=pl.BlockSpec((1, 8, HA * wv), per_b3),
        scratch_shapes=[pltpu.VMEM((2, rows, 1), F32), pltpu.VMEM((2, rows, 1), F32), pltpu.VMEM((2, rows, wv), F32)])
    return pl.pallas_call(
        functools.partial(_sdiff_body, n_steps=n_steps, gp=gp, lam_init=lam_init),
        grid_spec=grid_spec, out_shape=jax.ShapeDtypeStruct((bd, 8, HA * wv), F32),
        compiler_params=_cparams(("parallel", "arbitrary")),
        name="sample_diff_attention",
    )(page_table, lam, q_rows, *([kcache] * gp), *([vcache] * gp), knew, vnew, far_col, near_last, near_new,
      subln_g)


def _rwprep_body(x_ref, p8_ref, sh_ref, mix_ref, *o_refs):
    it = pl.program_id(1)
    x = x_ref[0]
    prev_last = jnp.where(it == 0, sh_ref[0], p8_ref[0, 7:8, :])
    row = lax.broadcasted_iota(I32, x.shape, 0)
    x_prev = jnp.where(row == 0, prev_last, pltpu.roll(x, 1, 0))
    xx = x_prev - x
    for c, o_ref in enumerate(o_refs):
        o_ref[0] = (x + xx * mix_ref[c:c + 1, :]).astype(o_ref.dtype)


def _rw_prep(x, shift0, mix):
    b, t, d = x.shape
    tt = _tile(t, 512)
    td = _tile(d, 1024)
    blk = pl.BlockSpec((1, tt, td), lambda b_, i, j: (b_, i, j))
    return pl.pallas_call(
        _rwprep_body, grid=(b, t // tt, d // td),
        in_specs=[blk,
                  pl.BlockSpec((1, 8, td), lambda b_, i, j: (b_, jnp.maximum(i * (tt // 8) - 1, 0), j)),
                  pl.BlockSpec((1, 1, td), lambda b_, i, j: (b_, 0, j)),
                  pl.BlockSpec((6, td), lambda b_, i, j: (0, j))],
        out_specs=[blk] * 6,
        out_shape=[jax.ShapeDtypeStruct((b, t, d), BF16)] * 6,
        compiler_params=_cparams(("parallel", "parallel", "parallel")),
        name="rwkv_token_shift_mix",
    )(x, x, shift0[:, None], mix)


def _scan_body(r_ref, w_ref, k_ref, v_ref, a_ref, g_ref, kk_ref, ka_ref, rk_ref, gg_ref, gb_ref, s0_ref,
               y_ref, sf_ref, st_ref, *, heads, c, nchunks, exact):
    ic = pl.program_id(2)

    @pl.when(ic == 0)
    def _init():
        st_ref[...] = s0_ref[0]

    if exact:
        def mm(a, b, dims=(((1,), (0,)), ((), ()))):
            return lax.dot_general(a, b, dims, precision=HIGHEST, preferred_element_type=F32)
    else:
        def mm(a, b, dims=(((1,), (0,)), ((), ()))):
            return lax.dot_general(a.astype(BF16), b.astype(BF16), dims, preferred_element_type=F32)
    nt = (((1,), (1,)), ((), ()))
    tn = (((0,), (0,)), ((), ()))
    hs = range(heads)
    sl = [slice(h * RW_N, (h + 1) * RW_N) for h in hs]

    row = lax.broadcasted_iota(I32, (c, 2 * c), 0)
    col = lax.broadcasted_iota(I32, (c, 2 * c), 1) % c
    strict = row > col
    incl = row >= col
    r_all, k_all, v_all, a_all = r_ref[0], k_ref[0], v_ref[0], a_ref[0]
    lw = -jnp.exp(w_ref[0])
    cum = jnp.dot(jnp.where(incl[:, :c], 1.0, 0.0), lw, precision=HIGHEST, preferred_element_type=F32)
    cum_end = cum[c - 1:c, :]
    kkraw = k_all * kk_ref[...]
    kmod = k_all * (1.0 + (a_all - 1.0) * ka_ref[...])
    e_neg = jnp.exp(-cum)
    e_end = jnp.exp(cum_end - cum)
    e_ex = jnp.exp(cum - lw)
    d_end = jnp.exp(cum_end)
    r_in = r_all * jnp.exp(cum)
    k_neg = kmod * e_neg
    k_end = kmod * e_end

    kk = []
    for h in hs:
        x = kkraw[:, sl[h]]
        kk.append(x * lax.rsqrt(jnp.maximum(jnp.sum(x * x, -1, keepdims=True), 1e-24)))
    bb = [kk[h] * a_all[:, sl[h]] for h in hs]
    v = [v_all[:, sl[h]] for h in hs]
    left = [jnp.concatenate([kk[h] * e_ex[:, sl[h]], r_in[:, sl[h]]], 0) for h in hs]
    right = [jnp.concatenate([bb[h] * e_neg[:, sl[h]], k_neg[:, sl[h]]], 0) for h in hs]
    s0 = [st_ref[h] for h in hs]
    amat = [mm(left[h], right[h], nt) for h in hs]
    ps = [mm(left[h], s0[h], nt) for h in hs]
    top = [jnp.where(strict, amat[h][:c], 0.0) for h in hs]
    bot = [jnp.where(incl, amat[h][c:], 0.0) for h in hs]
    x = [-(ps[h][:c] + mm(top[h][:, c:], v[h])) for h in hs]
    mp = [top[h][:, :c] for h in hs]
    x = [x[h] - mm(mp[h], x[h]) for h in hs]
    pw = 1
    while 2 * pw < c:
        mp = [mm(mp[h], mp[h]) for h in hs]
        pw *= 2
        x = [x[h] + mm(mp[h], x[h]) for h in hs]
    xv = [jnp.concatenate([x[h], v[h]], 0) for h in hs]
    o = [ps[h][c:] + mm(bot[h], xv[h]) for h in hs]
    upd = [jnp.concatenate([bb[h] * e_end[:, sl[h]], k_end[:, sl[h]]], 0) for h in hs]
    for h in hs:
        st_ref[h] = s0[h] * d_end[:, sl[h]] + mm(xv[h], upd[h], tn)
    ys = []
    for h in hs:
        mu = jnp.mean(o[h], -1, keepdims=True)
        oc = o[h] - mu
        var = jnp.mean(oc * oc, -1, keepdims=True)
        on = oc * lax.rsqrt(var + RW_GN_EPS) * gg_ref[:, sl[h]] + gb_ref[:, sl[h]]
        bonus = jnp.sum(r_all[:, sl[h]] * kmod[:, sl[h]] * rk_ref[:, sl[h]], -1, keepdims=True) * v[h]
        ys.append(on + bonus)
    y_ref[0] = (jnp.concatenate(ys, -1) * g_ref[0]).astype(y_ref.dtype)

    @pl.when(ic == nchunks - 1)
    def _finish():
        sf_ref[0] = st_ref[...]


def _rw_scan(r, w, k, v, a, g, k_k, k_a, r_k, gn_g, gn_b, state0, *, chunk, heads, exact):
    b, t, d = r.shape
    nh = d // RW_N
    c = _tile(t, chunk)
    heads = min(heads, nh)
    lanes = heads * RW_N
    seq = pl.BlockSpec((1, c, lanes), lambda b_, hg, ic: (b_, ic, hg))
    vec = pl.BlockSpec((1, lanes), lambda b_, hg, ic: (0, hg))
    st = pl.BlockSpec((1, heads, RW_N, RW_N), lambda b_, hg, ic: (b_, hg, 0, 0))
    return pl.pallas_call(
        functools.partial(_scan_body, heads=heads, c=c, nchunks=t // c, exact=exact),
        grid=(b, nh // heads, t // c),
        in_specs=[seq] * 6 + [vec] * 5 + [st],
        out_specs=[seq, st],
        out_shape=[jax.ShapeDtypeStruct((b, t, d), BF16), jax.ShapeDtypeStruct((b, nh, RW_N, RW_N), F32)],
        scratch_shapes=[pltpu.VMEM((heads, RW_N, RW_N), F32)],
        compiler_params=_cparams(("parallel", "parallel", "arbitrary")),
        name="rwkv_chunked_scan",
    )(r, w, k, v, a, g, k_k[None], k_a[None], r_k.reshape(1, d), gn_g[None], gn_b[None], state0)


def _route(x, w, b):
    logits = jnp.dot(x, w, precision=HIGHEST, preferred_element_type=F32)
    scores = jax.nn.sigmoid(logits)
    sel = scores + b
    sc = [sel[:, e:e + 1] for e in range(N_EXPERTS)]
    raw = [scores[:, e:e + 1] for e in range(N_EXPERTS)]
    best = jnp.zeros_like(sc[0]).astype(I32)
    best_score = None
    for gi in range(N_GROUPS):
        m = sc[gi * EXP_PER_GROUP:(gi + 1) * EXP_PER_GROUP]
        pair = None
        for i in range(EXP_PER_GROUP):
            for j in range(i + 1, EXP_PER_GROUP):
                pair = m[i] + m[j] if pair is None else jnp.maximum(pair, m[i] + m[j])
        if best_score is None:
            best_score = pair
        else:
            upd = pair > best_score
            best = jnp.where(upd, gi, best)
            best_score = jnp.where(upd, pair, best_score)

    def pick(vals, i):
        out = vals[i]
        for gi in range(1, N_GROUPS):
            out = jnp.where(best == gi, vals[gi * EXP_PER_GROUP + i], out)
        return out

    a = [pick(sc, i) for i in range(EXP_PER_GROUP)]
    cw = [pick(raw, i) for i in range(EXP_PER_GROUP)]
    i1, v1, w1 = jnp.zeros_like(best), a[0], cw[0]
    for i in range(1, EXP_PER_GROUP):
        upd = a[i] > v1
        i1, v1, w1 = jnp.where(upd, i, i1), jnp.where(upd, a[i], v1), jnp.where(upd, cw[i], w1)
    i2, v2, w2 = None, None, None
    for i in range(EXP_PER_GROUP):
        ok = i1 != i
        if v2 is None:
            i2, v2, w2 = jnp.full_like(best, i), jnp.where(ok, a[i], -jnp.inf), cw[i]
        else:
            upd = jnp.logical_and(ok, a[i] > v2)
            i2, v2, w2 = jnp.where(upd, i, i2), jnp.where(upd, a[i], v2), jnp.where(upd, cw[i], w2)
    tot = w1 + w2
    e1 = best * EXP_PER_GROUP + i1
    e2 = best * EXP_PER_GROUP + i2
    lane = lax.broadcasted_iota(I32, scores.shape, 1)
    gates = jnp.where(lane == e1, w1 / tot, 0.0) + jnp.where(lane == e2, w2 / tot, 0.0)
    return gates, best


def _router_body(x_ref, w_ref, b_ref, *o_refs, d, extended):
    x = x_ref[...]
    gates, best = _route(x, w_ref[...], b_ref[...])
    if extended:
        xe_ref, grp_ref = o_refs
        xe_ref[:, :d] = x
        xe_ref[:, d:] = gates
        grp_ref[...] = best
    else:
        o_refs[0][...] = gates


def _router(x, w_router, router_bias, extended):
    n, d = x.shape
    tm = _tile(n, 512)
    if extended:
        out_specs = [pl.BlockSpec((tm, d + LANE), lambda i: (i, 0)), pl.BlockSpec((tm, 1), lambda i: (i, 0))]
        out_shape = [jax.ShapeDtypeStruct((n, d + LANE), F32), jax.ShapeDtypeStruct((n, 1), I32)]
    else:
        out_specs = pl.BlockSpec((tm, LANE), lambda i: (i, 0))
        out_shape = jax.ShapeDtypeStruct((n, LANE), F32)
    return pl.pallas_call(
        functools.partial(_router_body, d=d, extended=extended), grid=(n // tm,),
        in_specs=[pl.BlockSpec((tm, d), lambda i: (i, 0)),
                  pl.BlockSpec((d, LANE), lambda i: (0, 0)),
                  pl.BlockSpec((1, LANE), lambda i: (0, 0))],
        out_specs=out_specs, out_shape=out_shape,
        compiler_params=_cparams(("parallel",)),
        name="moe_router",
    )(x, jnp.pad(w_router, ((0, 0), (0, LANE - N_EXPERTS))),
      jnp.pad(router_bias, (0, LANE - N_EXPERTS))[None])


def _gate_column(gates, expert):
    lane = lax.broadcasted_iota(I32, gates.shape, 1)
    return jnp.sum(jnp.where(lane == expert, gates, 0.0), -1, keepdims=True)


def _experts_body(x_ref, gt_ref, wg_ref, wu_ref, wd_ref, o_ref):
    e, f = pl.program_id(1), pl.program_id(2)

    @pl.when(jnp.logical_and(e == 0, f == 0))
    def _init():
        o_ref[...] = jnp.zeros_like(o_ref)

    x = x_ref[...]
    hg = _dot(x, wg_ref[0, 0])
    hu = _dot(x, wu_ref[0, 0])
    hid = hg * jax.nn.sigmoid(hg) * hu * _gate_column(gt_ref[...], e)
    o_ref[...] += _dot(hid, wd_ref[0, 0])


def _experts_dense(x, gates, w_gate, w_up, w_down, layer):
    n, d = x.shape
    ff = w_gate.shape[-1]
    tm = _tile(n, 512)
    tf = _tile(ff, 256)
    return pl.pallas_call(
        _experts_body, grid=(n // tm, N_EXPERTS, ff // tf),
        in_specs=[pl.BlockSpec((tm, d), lambda i, e, f: (i, 0)),
                  pl.BlockSpec((tm, LANE), lambda i, e, f: (i, 0)),
                  pl.BlockSpec((1, 1, d, tf), lambda i, e, f: (layer, e, 0, f)),
                  pl.BlockSpec((1, 1, d, tf), lambda i, e, f: (layer, e, 0, f)),
                  pl.BlockSpec((1, 1, tf, d), lambda i, e, f: (layer, e, f, 0))],
        out_specs=pl.BlockSpec((tm, d), lambda i, e, f: (i, 0)),
        out_shape=jax.ShapeDtypeStruct((n, d), F32),
        compiler_params=_cparams(("parallel", "arbitrary", "arbitrary")),
        name="moe_experts_dense",
    )(x, gates, w_gate, w_up, w_down)


def _row_copy(tab_ref, buf_ref, sem, r, idx):
    return pltpu.make_async_copy(tab_ref.at[pl.ds(idx, 1)], buf_ref.at[pl.ds(r, 1)], sem.at[0])


def _gather_rows(idx_ref, tab_ref, buf_ref, sem, rows):
    base = pl.program_id(0) * rows

    def issue(r, carry):
        _row_copy(tab_ref, buf_ref, sem, r, idx_ref[base + r]).start()
        return carry

    def wait(r, carry):
        _row_copy(tab_ref, buf_ref, sem, r, 0).wait()
        return carry

    lax.fori_loop(0, rows, issue, 0, unroll=8)
    lax.fori_loop(0, rows, wait, 0, unroll=8)


def _sort_gather_body(src_ref, tab_ref, xs_ref, gs_ref, buf_ref, sem, *, rows, d):
    _gather_rows(src_ref, tab_ref, buf_ref, sem, rows)
    xs_ref[...] = buf_ref[:, :d].astype(BF16)
    gs_ref[...] = buf_ref[:, d:]


def _sort_gather(x_ext, src, rows):
    p = src.shape[0]
    d = x_ext.shape[1] - LANE
    grid_spec = pltpu.PrefetchScalarGridSpec(
        num_scalar_prefetch=1, grid=(p // rows,),
        in_specs=[pl.BlockSpec(memory_space=pl.ANY)],
        out_specs=[pl.BlockSpec((rows, d), lambda i, s: (i, 0)), pl.BlockSpec((rows, LANE), lambda i, s: (i, 0))],
        scratch_shapes=[pltpu.VMEM((rows, d + LANE), F32), pltpu.SemaphoreType.DMA((1,))])
    return pl.pallas_call(
        functools.partial(_sort_gather_body, rows=rows, d=d), grid_spec=grid_spec,
        out_shape=[jax.ShapeDtypeStruct((p, d), BF16), jax.ShapeDtypeStruct((p, LANE), F32)],
        compiler_params=_cparams(("arbitrary",)),
        name="moe_sort_gather",
    )(src, x_ext)


def _unsort_ln_body(dest_ref, ys_ref, x_ref, g_ref, b_ref, o_ref, buf_ref, sem, *, rows):
    _gather_rows(dest_ref, ys_ref, buf_ref, sem, rows)
    o_ref[...] = _ln_rows(ALPHA * x_ref[...] + buf_ref[...], g_ref[...], b_ref[...])


def _unsort_ln(y_sorted, dest, x, g, b, rows):
    n, d = x.shape
    grid_spec = pltpu.PrefetchScalarGridSpec(
        num_scalar_prefetch=1, grid=(n // rows,),
        in_specs=[pl.BlockSpec(memory_space=pl.ANY),
                  pl.BlockSpec((rows, d), lambda i, s: (i, 0)),
                  pl.BlockSpec((1, d), lambda i, s: (0, 0)),
                  pl.BlockSpec((1, d), lambda i, s: (0, 0))],
        out_specs=pl.BlockSpec((rows, d), lambda i, s: (i, 0)),
        scratch_shapes=[pltpu.VMEM((rows, d), F32), pltpu.SemaphoreType.DMA((1,))])
    return pl.pallas_call(
        functools.partial(_unsort_ln_body, rows=rows), grid_spec=grid_spec,
        out_shape=jax.ShapeDtypeStruct((n, d), F32),
        compiler_params=_cparams(("arbitrary",)),
        name="moe_unsort_ln",
    )(dest, y_sorted, x, g[None], b[None])


def _up_body(tg_ref, nv_ref, x_ref, gt_ref, wg_ref, wu_ref, h_ref):
    j, t = pl.program_id(1), pl.program_id(2)

    @pl.when(t < nv_ref[0])
    def _compute():
        x = x_ref[...]
        hg = _dot(x, wg_ref[0, 0])
        hu = _dot(x, wu_ref[0, 0])
        gate = _gate_column(gt_ref[...], tg_ref[t] * EXP_PER_GROUP + j)
        h_ref[...] = (hg * jax.nn.sigmoid(hg) * hu * gate).astype(h_ref.dtype)

    @pl.when(t >= nv_ref[0])
    def _unused_tile():
        h_ref[...] = jnp.zeros_like(h_ref)


def _experts_up(xs, gs, tile_group, n_valid, w_gate, w_up, layer, tm):
    p, d = xs.shape
    ff = w_gate.shape[-1]
    tf = _tile(ff, 512)
    nf = ff // tf
    wspec = pl.BlockSpec((1, 1, d, tf), lambda f, j, t, tg, nv: (layer, tg[t] * EXP_PER_GROUP + j, 0, f))
    grid_spec = pltpu.PrefetchScalarGridSpec(
        num_scalar_prefetch=2, grid=(nf, EXP_PER_GROUP, p // tm),
        in_specs=[pl.BlockSpec((tm, d), lambda f, j, t, tg, nv: (t, 0)),
                  pl.BlockSpec((tm, LANE), lambda f, j, t, tg, nv: (t, 0)),
                  wspec, wspec],
        out_specs=pl.BlockSpec((tm, tf), lambda f, j, t, tg, nv: (t, j * nf + f)))
    return pl.pallas_call(
        _up_body, grid_spec=grid_spec,
        out_shape=jax.ShapeDtypeStruct((p, EXP_PER_GROUP * ff), BF16),
        compiler_params=_cparams(("arbitrary", "arbitrary", "arbitrary")),
        name="moe_experts_up",
    )(tile_group, n_valid, xs, gs, w_gate, w_up)


def _down_body(tg_ref, nv_ref, h_ref, wd_ref, o_ref):
    t, j = pl.program_id(0), pl.program_id(2)

    @pl.when(j == 0)
    def _init():
        o_ref[...] = jnp.zeros_like(o_ref)

    @pl.when(t < nv_ref[0])
    def _compute():
        o_ref[...] += _dot(h_ref[...], wd_ref[0, 0])


def _experts_down(h, tile_group, n_valid, w_down, layer, tm):
    p = h.shape[0]
    ff, d = w_down.shape[-2:]
    tn = _tile(d, 2048)
    grid_spec = pltpu.PrefetchScalarGridSpec(
        num_scalar_prefetch=2, grid=(p // tm, d // tn, EXP_PER_GROUP),
        in_specs=[pl.BlockSpec((tm, ff), lambda t, n, j, tg, nv: (t, j)),
                  pl.BlockSpec((1, 1, ff, tn),
                               lambda t, n, j, tg, nv: (layer, tg[t] * EXP_PER_GROUP + j, 0, n))],
        out_specs=pl.BlockSpec((tm, tn), lambda t, n, j, tg, nv: (t, n)))
    return pl.pallas_call(
        _down_body, grid_spec=grid_spec,
        out_shape=jax.ShapeDtypeStruct((p, d), F32),
        compiler_params=_cparams(("arbitrary", "arbitrary", "arbitrary")),
        name="moe_experts_down",
    )(tile_group, n_valid, h, w_down)


def _moe_ln_grouped(x, w_router, router_bias, w_gate, w_up, w_down, layer, g, b, tm=512):
    n, d = x.shape
    x_ext, grp = _router(x, w_router, router_bias, True)
    grp = grp[:, 0]
    onehot = (grp[:, None] == jnp.arange(N_GROUPS, dtype=I32)[None]).astype(I32)
    counts = jnp.sum(onehot, 0)
    ntile = (counts + tm - 1) // tm
    tile_end = jnp.cumsum(ntile)
    tile_start = tile_end - ntile
    rank = jnp.sum(jnp.cumsum(onehot, 0) * onehot, 1) - 1
    dest = (tile_start[grp] * tm + rank).astype(I32)
    n_tiles = n // tm + N_GROUPS - 1
    src = jnp.zeros((n_tiles * tm,), I32).at[dest].set(jnp.arange(n, dtype=I32))
    n_valid = tile_end[-1:].astype(I32)
    tile_ids = jnp.minimum(jnp.arange(n_tiles, dtype=I32), n_valid[0] - 1)
    tile_group = jnp.sum((tile_ids[:, None] >= tile_end[None, :]).astype(I32), 1).astype(I32)
    xs, gs = _sort_gather(x_ext, src, 256)
    h = _experts_up(xs, gs, tile_group, n_valid, w_gate, w_up, layer, tm)
    ys = _experts_down(h, tile_group, n_valid, w_down, layer, tm)
    return _unsort_ln(ys, dest, x, g, b, 256)


def _moe_ln_dense(x, w_router, router_bias, w_gate, w_up, w_down, layer, g, b):
    gates = _router(x, w_router, router_bias, False)
    return _ln_res(x, _experts_dense(x.astype(BF16), gates, w_gate, w_up, w_down, layer), g, b)


def _project_att(x2, w_in, j):
    mm = functools.partial(_matmul, x2, w_in, j)
    wa = HA * 2 * DH
    qa = mm(col0=C_QA, ncols=wa, out_dtype=BF16)
    ka = mm(col0=C_KA, ncols=wa)
    va = mm(col0=C_VA, ncols=wa)
    qb = mm(col0=C_QB, ncols=HB * DH, out_dtype=BF16)
    kb = mm(col0=C_KB, ncols=KVB * DH)
    vb = mm(col0=C_VB, ncols=KVB * DH)
    qi = mm(col0=C_QI, ncols=H_IDX * D_IDX, out_dtype=BF16, tn=512)
    kw = mm(col0=C_KI, ncols=D_IDX + H_IDX)
    return qa, ka, va, qb, kb, vb, qi, kw[:, :D_IDX], kw[:, D_IDX:]


def _att_prompt(x, w_in, w_out, j, rel_bias, lam, lam_init, subln_g):
    b, s, d = x.shape
    x2 = x.reshape(b * s, d).astype(BF16)
    qa, ka, va, qb, kb, vb, qi, ki, wi = _project_att(x2, w_in, j)
    t = min(512, max(s // 2, 8))
    r = np.arange(t)
    dist = r[:, None] - r[None, :]
    tiles = _bias_tiles(rel_bias, np.stack([_t5_bucket_np(dist), _t5_bucket_np(dist + t)]))
    sh = lambda z: z.reshape(b, s, z.shape[-1])
    oa = _prompt_diff(sh(qa), sh(ka), sh(va), tiles[:HA], rel_bias, lam, subln_g, lam_init, t)
    mask = _prompt_select(sh(qi), sh(ki), sh(wi), min(TOPK_MAX, s // 4))
    ob = _prompt_sparse(sh(qb), sh(kb), sh(vb), mask, tiles[HA:], rel_bias, t)
    o = jnp.concatenate([oa, ob], -1).reshape(b * s, -1)
    y = _matmul(o, w_out, j)
    return y, (ka, va, kb, vb, ki)


def _att_sample(x, j, cache_ka, cache_va, cache_kb, cache_vb, cache_kidx, page_table,
                w_in, w_out, rel_bias, lam, lam_init, subln_g):
    bd, t, d = x.shape
    assert t == 8
    n_pages = page_table.shape[1]
    gp = 4 if n_pages % 4 == 0 else 1
    past_len = n_pages * PAGE
    na, n_pool = cache_ka.shape[:2]
    x2 = x.reshape(bd * t, d).astype(BF16)
    qa, ka, va, qb, kb, vb, qi, ki, wi = _project_att(x2, w_in, j)

    def pad_rows(z):
        return jnp.pad(z.reshape(bd, t, z.shape[-1]), ((0, 0), (0, PAGE - t), (0, 0)))

    tt = np.arange(t)[:, None]
    cc = np.arange(PAGE)[None, :]
    tiles = _bias_tiles(rel_bias, np.stack([_t5_bucket_np(PAGE + tt - cc), _t5_bucket_np(tt - cc)]))
    far = rel_bias[N_BUCKETS - 1]

    qa_rows = qa.reshape(bd, t, HA, 2, DH).transpose(0, 3, 2, 1, 4).reshape(bd, 2, HA * t, DH)
    ka_rows = ka.reshape(bd, t, HA, 2, DH).transpose(0, 3, 1, 2, 4).reshape(bd, 2, t * HA, DH)
    near_last = jnp.repeat(tiles[:HA, 0].reshape(HA * t, PAGE), HA, axis=1)
    near_new = jnp.repeat(tiles[:HA, 1, :, :t].reshape(HA * t, t), HA, axis=1)
    far_a = jnp.repeat(far[:HA], t)[:, None]
    oa = _sample_diff(page_table, qa_rows, cache_ka.reshape(na, n_pool, PAGE * HA * 2, DH),
                      cache_va.reshape(na, n_pool, PAGE * HA, 2 * DH), ka_rows, va.reshape(bd, t * HA, 2 * DH),
                      far_a, near_last, near_new, lam, subln_g, lam_init, j, gp)

    qi_rows = qi.reshape(bd, t, H_IDX, D_IDX).transpose(0, 2, 1, 3).reshape(bd, H_IDX * t, D_IDX)
    wi_col = wi.reshape(bd, t, H_IDX).transpose(0, 2, 1).reshape(bd, H_IDX * t, 1)
    sc = _sample_index(page_table, qi_rows, wi_col, cache_kidx, j, gp)
    mask_c, mask_n = _sample_select(sc, qi_rows, wi_col, pad_rows(ki), min(TOPK_MAX, (past_len + t) // 4))

    qb_rows = qb.reshape(bd, t, HB, DH).transpose(0, 2, 1, 3).reshape(bd, HB * t, DH)
    near_b = tiles[HA:].transpose(1, 0, 2, 3).reshape(2, HB * t, PAGE)
    far_b = jnp.repeat(far[HA:], t)[:, None]
    ob = _sample_sparse(page_table, qb_rows, cache_kb.reshape(na, n_pool, PAGE * KVB, DH),
                        cache_vb.reshape(na, n_pool, PAGE * KVB, DH), pad_rows(kb), pad_rows(vb),
                        far_b, near_b, mask_c, mask_n, j, gp)
    ob = ob.reshape(bd, HB, t, DH).transpose(0, 2, 1, 3).reshape(bd, t, HB * DH)
    o = jnp.concatenate([oa, ob], -1).reshape(bd * t, -1).astype(BF16)
    y = _matmul(o, w_out, j)
    return y, (ka, va, kb, vb, ki)


def _softplus(z):
    return jnp.maximum(z, 0.0) + jnp.log(1.0 + jnp.exp(-jnp.abs(z)))


def _rwkv(x, shift0, wkv0, j, p, *, chunk, heads, exact):
    b, t, d = x.shape
    xr, xw, xk, xv, xa, xg = [z.reshape(b * t, d) for z in _rw_prep(x, shift0, p['mix'][j])]
    r = _matmul(xr, p['wr'], j)
    k = _matmul(xk, p['wk'], j)
    v = _matmul(xv, p['wv'], j)
    w = _matmul(_matmul(xw, p['w1'], j, epilogue=jnp.tanh), p['w2'], j,
                epilogue=lambda acc, w0: -_softplus(-(w0 + acc)) - 0.5, extras=(p['w0'][j][None],))
    a = _matmul(_matmul(xa, p['a1'], j), p['a2'], j,
                epilogue=lambda acc, a0: jax.nn.sigmoid(a0 + acc), extras=(p['a0'][j][None],))
    g = _matmul(_matmul(xg, p['g1'], j, epilogue=jax.nn.sigmoid), p['g2'], j)
    sh = lambda z: z.reshape(b, t, d)
    y_pre, s_fin = _rw_scan(sh(r), sh(w), sh(k), sh(v), sh(a), sh(g), p['kk'][j], p['ka'][j], p['rk'][j],
                            p['gn_g'][j], p['gn_b'][j], wkv0, chunk=chunk, heads=heads, exact=exact)
    y = _matmul(y_pre.reshape(b * t, d), p['wo'], j)
    return y, s_fin, x[:, -1]


def kernel(x_prompt, x_sample, cache_ka, cache_va, cache_kb, cache_vb, cache_kidx, state_wkv, state_shift,
           page_table, rel_bias, w_in_att, w_out_att, lam_q1, lam_k1, lam_q2, lam_k2, subln_g,
           rw_mix, rw_w0, rw_w1, rw_w2, rw_a0, rw_a1, rw_a2, rw_g1, rw_g2, rw_kk, rw_ka, rw_rk,
           rw_wr, rw_wk, rw_wv, rw_wo, rw_gn_g, rw_gn_b, w_router, router_bias, w_gate, w_up, w_down,
           ln1_g, ln1_b, ln2_g, ln2_b):
    bp, s, d = x_prompt.shape
    bd, t, _ = x_sample.shape
    rw = dict(mix=rw_mix, w0=rw_w0, w1=rw_w1, w2=rw_w2, a0=rw_a0, a1=rw_a1, a2=rw_a2, g1=rw_g1, g2=rw_g2,
              kk=rw_kk, ka=rw_ka, rk=rw_rk, wr=rw_wr, wk=rw_wk, wv=rw_wv, wo=rw_wo, gn_g=rw_gn_g, gn_b=rw_gn_b)
    moe_w = (w_router, router_bias, w_gate, w_up, w_down)
    yp, ys = x_prompt, x_sample
    rows_p, rows_s = [], []
    wkv_p, shift_p, wkv_s, shift_s = [], [], [], []
    n_layers = w_gate.shape[0]
    for i in range(n_layers):
        j = i // 2
        if i % 2 == 0:
            lam_init = 0.8 - 0.6 * math.exp(-0.3 * i)
            lam = jnp.stack([lam_q1[j], lam_k1[j], lam_q2[j], lam_k2[j]], 0)
            mp, rp = _att_prompt(yp, w_in_att, w_out_att, j, rel_bias, lam, lam_init, subln_g[j][None])
            ms, rs = _att_sample(ys, j, cache_ka, cache_va, cache_kb, cache_vb, cache_kidx, page_table,
                                 w_in_att, w_out_att, rel_bias, lam, lam_init, subln_g[j][None])
            rows_p.append(rp)
            rows_s.append(rs)
        else:
            nh = d // RW_N
            mp, sp, hp = _rwkv(yp, jnp.zeros((bp, d), F32), jnp.zeros((bp, nh, RW_N, RW_N), F32), j, rw,
                               chunk=64, heads=16, exact=False)
            ms, ss, hs = _rwkv(ys, state_shift[j], state_wkv[j], j, rw, chunk=8, heads=8, exact=True)
            wkv_p.append(sp)
            shift_p.append(hp)
            wkv_s.append(ss)
            shift_s.append(hs)
        yp2 = _ln_res(yp.reshape(bp * s, d), mp, ln1_g[i], ln1_b[i])
        ys2 = _ln_res(ys.reshape(bd * t, d), ms, ln1_g[i], ln1_b[i])
        yp2 = _moe_ln_grouped(yp2, *moe_w, i, ln2_g[i], ln2_b[i])
        ys2 = _moe_ln_dense(ys2, *moe_w, i, ln2_g[i], ln2_b[i])
        yp, ys = yp2.reshape(bp, s, d), ys2.reshape(bd, t, d)

    def rows(rs_list, b_, t_):
        ka, va, kb, vb, ki = (jnp.stack(z, 0) for z in zip(*rs_list))
        n = ka.shape[0]
        return (ka.reshape(n, b_, t_, HA, 2, DH), va.reshape(n, b_, t_, HA, 2 * DH),
                kb.reshape(n, b_, t_, KVB, DH), vb.reshape(n, b_, t_, KVB, DH), ki.reshape(n, b_, t_, D_IDX))

    return (yp, ys) + rows(rows_p, bp, s) + rows(rows_s, bd, t) + (
        jnp.stack(wkv_p, 0), jnp.stack(shift_p, 0), jnp.stack(wkv_s, 0), jnp.stack(shift_s, 0))
```
